```python
import math, functools
import jax, jax.numpy as jnp
from jax import lax
import numpy as np

D_MODEL = 2048
BATCH = 2
SEQ = 4096
DEPTH = 1
DEC_BATCH = 128
DEC_SEQ = 8
PAST_LEN = 8192
PAGE_SIZE = 128

MLA_HEADS = 8
Q_LORA = 512
KV_LORA = 512
NOPE_DIM = 128
ROPE_DIM = 64
V_DIM = 128
Q_BLOCK = 128
RET_HEADS = 8
RET_DK = 64
RET_DV = 128
RET_CHUNK = 128
MIX_WIDTH = MLA_HEADS * V_DIM + RET_HEADS * RET_DV
SPLITS = (Q_LORA, KV_LORA, ROPE_DIM, RET_HEADS * RET_DK, RET_HEADS * RET_DK,
          RET_HEADS * RET_DV, RET_HEADS * RET_DV)
SPLIT_POINTS = tuple(int(s) for s in np.cumsum(SPLITS)[:-1])
PROJ_WIDTH = sum(SPLITS)
N_GROUPS = 4
EXPERTS_PER_GROUP = 4
N_EXPERTS = N_GROUPS * EXPERTS_PER_GROUP
TOP_K = 2
D_EXPERT = 704
ROPE_BASE = 10000.0
EPS = 1e-6

kernel_name = "hymba_mla_retention_hmoe_step"


def _rmsnorm(x, g):
    xf = x.astype(jnp.float32)
    y = xf * lax.rsqrt(jnp.mean(xf * xf, axis=-1, keepdims=True) + EPS)
    return (y * g.astype(jnp.float32)).astype(x.dtype)


def _rope(x, pos):
    r = x.shape[-1]
    inv = 1.0 / (ROPE_BASE ** (jnp.arange(0, r, 2, dtype=jnp.float32) / r))
    ang = pos.astype(jnp.float32)[:, None] * inv[None, :]
    cos = jnp.cos(ang)[None, :, None, :].astype(x.dtype)
    sin = jnp.sin(ang)[None, :, None, :].astype(x.dtype)
    x1, x2 = x[..., : r // 2], x[..., r // 2:]
    return jnp.concatenate([x1 * cos - x2 * sin, x2 * cos + x1 * sin], axis=-1)


def _mla_scores(q_lat, q_pe, c, kpe):
    s = jnp.einsum('bqhc,bkc->bhqk', q_lat, c) + jnp.einsum('bqhr,bkr->bhqk', q_pe, kpe)
    return s.astype(jnp.float32) * (NOPE_DIM + ROPE_DIM) ** -0.5


def _mla_prompt(q_lat, q_pe, c, kpe):
    b, s, h, cdim = q_lat.shape
    qb = math.gcd(s, Q_BLOCK)
    nb = s // qb
    ql = q_lat.reshape(b, nb, qb, h, cdim).swapaxes(0, 1)
    qp = q_pe.reshape(b, nb, qb, h, ROPE_DIM).swapaxes(0, 1)
    kpos = jnp.arange(s)

    def blk(args):
        i, qli, qpi = args
        sc = _mla_scores(qli, qpi, c, kpe)
        qpos = i * qb + jnp.arange(qb)
        sc = jnp.where(kpos[None, :] <= qpos[:, None], sc, -jnp.inf)
        p = jax.nn.softmax(sc, axis=-1).astype(c.dtype)
        return jnp.einsum('bhqk,bkc->bqhc', p, c)

    out = lax.map(blk, (jnp.arange(nb), ql, qp))
    return out.swapaxes(0, 1).reshape(b, s, h, cdim)


def _mla_sample(q_lat, q_pe, c, kpe, c_past, kpe_past):
    t = q_lat.shape[1]
    p_len = c_past.shape[1]
    s_past = _mla_scores(q_lat, q_pe, c_past, kpe_past)
    s_new = _mla_scores(q_lat, q_pe, c, kpe)
    causal = jnp.tril(jnp.ones((t, t), dtype=bool))
    s_new = jnp.where(causal, s_new, -jnp.inf)
    p = jax.nn.softmax(jnp.concatenate([s_past, s_new], axis=-1), axis=-1).astype(c.dtype)
    return (jnp.einsum('bhqk,bkc->bqhc', p[..., :p_len], c_past)
            + jnp.einsum('bhqk,bkc->bqhc', p[..., p_len:], c))


def _retention(q, k, v, s0):
    b, t, h, dk = q.shape
    dv = v.shape[-1]
    l = math.gcd(t, RET_CHUNK)
    nc = t // l
    log_g = jnp.log1p(-jnp.exp2(-5.0 - jnp.arange(h, dtype=jnp.float32)))
    i = jnp.arange(l, dtype=jnp.float32)
    diff = i[:, None] - i[None, :]
    d_intra = jnp.where(diff >= 0, jnp.exp(jnp.maximum(diff, 0.0)[None] * log_g[:, None, None]),
                        0.0).astype(q.dtype)
    d_q = jnp.exp((i[:, None] + 1.0) * log_g[None, :]).astype(q.dtype)
    d_k = jnp.exp((l - 1.0 - i)[:, None] * log_g[None, :]).astype(q.dtype)
    d_s = jnp.exp(l * log_g).astype(q.dtype)

    def step(s, blk):
        qc, kc, vc = blk
        a = jnp.einsum('bihd,bjhd->bhij', qc, kc) * d_intra
        o = (jnp.einsum('bhij,bjhe->bihe', a, vc)
             + jnp.einsum('bihd,bhde->bihe', qc, s) * d_q[None, :, :, None])
        s_new = (s * d_s[None, :, None, None]
                 + jnp.einsum('bjhd,bjhe->bhde', kc * d_k[None, :, :, None], vc)).astype(s.dtype)
        return s_new, o

    def chunks(a):
        return a.reshape(b, nc, l, h, a.shape[-1]).swapaxes(0, 1)

    s_fin, o = lax.scan(step, s0, (chunks(q), chunks(k), chunks(v)))
    return o.swapaxes(0, 1).reshape(b, t, h, dv), s_fin


def _moe(x, w_rg, w_re, w_gate, w_up, w_down):
    lead = x.shape[:-1]
    xf = x.reshape(-1, x.shape[-1])
    lg = (xf @ w_rg).astype(jnp.float32)
    pg = jax.nn.softmax(lg, axis=-1)
    gsel = jnp.argmax(lg, axis=-1)
    gate_g = jnp.take_along_axis(pg, gsel[:, None], axis=-1)
    le = (xf @ w_re).astype(jnp.float32).reshape(-1, N_GROUPS, EXPERTS_PER_GROUP)
    le_sel = jnp.take_along_axis(le, gsel[:, None, None], axis=1)[:, 0]
    pe = jax.nn.softmax(le_sel, axis=-1)
    top_w, top_i = lax.top_k(pe, TOP_K)
    top_w = top_w / jnp.sum(top_w, axis=-1, keepdims=True) * gate_g
    eid = gsel[:, None] * EXPERTS_PER_GROUP + top_i
    combine = jnp.sum(jax.nn.one_hot(eid, N_EXPERTS, dtype=jnp.float32) * top_w[..., None], axis=1)
    a = jax.nn.silu(jnp.einsum('td,edf->tef', xf, w_gate)) * jnp.einsum('td,edf->tef', xf, w_up)
    out = jnp.einsum('tef,efd->td', a * combine[..., None].astype(a.dtype), w_down)
    return out.reshape(*lead, x.shape[-1])


def _layer(x, pos, p, attend, s0):
    b, t, _ = x.shape
    h = _rmsnorm(x, p['g_attn'])
    q_a, kv_a, k_pe, rq, rk, rv, rg = jnp.split(h @ p['w_in'], SPLIT_POINTS, axis=-1)
    q = jnp.einsum('bsc,chd->bshd', _rmsnorm(q_a, p['g_q_a']), p['w_q_b'])
    q_nope, q_pe = q[..., :NOPE_DIM], _rope(q[..., NOPE_DIM:], pos)
    c = _rmsnorm(kv_a, p['g_kv_a'])
    kpe = _rope(k_pe[:, :, None, :], pos)[:, :, 0, :]
    q_lat = jnp.einsum('bshn,chn->bshc', q_nope, p['w_uk'])
    o_lat = attend(q_lat, q_pe, c, kpe)
    o_mla = jnp.einsum('bshc,chv->bshv', o_lat, p['w_uv']).reshape(b, t, -1)
    rq = _rope(rq.reshape(b, t, RET_HEADS, RET_DK), pos) * (RET_DK ** -0.5)
    rk = _rope(rk.reshape(b, t, RET_HEADS, RET_DK), pos)
    rv = rv.reshape(b, t, RET_HEADS, RET_DV)
    o_ret, s_new = _retention(rq, rk, rv, s0)
    o_ret = _rmsnorm(o_ret, p['g_ret']).reshape(b, t, -1)
    o_ret = jax.nn.silu(rg) * o_ret
    x = x + jnp.concatenate([o_mla, o_ret], axis=-1) @ p['w_o']
    x = x + _moe(_rmsnorm(x, p['g_ffn']), p['w_router_group'], p['w_router_expert'],
                 p['w_exp_gate'], p['w_exp_up'], p['w_exp_down'])
    return x, c, kpe, s_new


def setup_inputs(seed: int = 0) -> dict:
    key = jax.random.key(seed)
    ks = jax.random.split(key, 24)
    f32 = jnp.float32

    def nrm(k, shape, scale):
        return jax.random.normal(k, shape, f32) * scale

    def gain(k, shape):
        return 1.0 + 0.05 * jax.random.normal(k, shape, f32)

    n_pages = PAST_LEN // PAGE_SIZE
    n_pool = (DEC_BATCH * n_pages * 5) // 4
    page_table = jax.random.permutation(ks[5], n_pool)[: DEC_BATCH * n_pages]
    page_table = page_table.reshape(DEC_BATCH, n_pages).astype(jnp.int32)
    return {
        'x_prompt': nrm(ks[0], (BATCH, SEQ, D_MODEL), 1.0),
        'x_sample': nrm(ks[1], (DEC_BATCH, DEC_SEQ, D_MODEL), 1.0),
        'cache_latent': nrm(ks[2], (DEPTH, n_pool, PAGE_SIZE, KV_LORA), 1.0),
        'cache_kpe': nrm(ks[3], (DEPTH, n_pool, PAGE_SIZE, ROPE_DIM), 1.0),
        'state_ret': nrm(ks[4], (DEPTH, DEC_BATCH, RET_HEADS, RET_DK, RET_DV), 1.0),
        'page_table': page_table,
        'g_attn': gain(ks[6], (DEPTH, D_MODEL)),
        'w_in': nrm(ks[7], (DEPTH, D_MODEL, PROJ_WIDTH), D_MODEL ** -0.5),
        'g_q_a': gain(ks[8], (DEPTH, Q_LORA)),
        'w_q_b': nrm(ks[9], (DEPTH, Q_LORA, MLA_HEADS, NOPE_DIM + ROPE_DIM), Q_LORA ** -0.5),
        'g_kv_a': gain(ks[10], (DEPTH, KV_LORA)),
        'w_uk': nrm(ks[11], (DEPTH, KV_LORA, MLA_HEADS, NOPE_DIM), KV_LORA ** -0.5),
        'w_uv': nrm(ks[12], (DEPTH, KV_LORA, MLA_HEADS, V_DIM), KV_LORA ** -0.5),
        'g_ret': gain(ks[13], (DEPTH, RET_HEADS, RET_DV)),
        'w_o': nrm(ks[14], (DEPTH, MIX_WIDTH, D_MODEL), MIX_WIDTH ** -0.5),
        'g_ffn': gain(ks[15], (DEPTH, D_MODEL)),
        'w_router_group': nrm(ks[16], (DEPTH, D_MODEL, N_GROUPS), D_MODEL ** -0.5),
        'w_router_expert': nrm(ks[17], (DEPTH, D_MODEL, N_EXPERTS), D_MODEL ** -0.5),
        'w_exp_gate': nrm(ks[18], (DEPTH, N_EXPERTS, D_MODEL, D_EXPERT), D_MODEL ** -0.5),
        'w_exp_up': nrm(ks[19], (DEPTH, N_EXPERTS, D_MODEL, D_EXPERT), D_MODEL ** -0.5),
        'w_exp_down': nrm(ks[20], (DEPTH, N_EXPERTS, D_EXPERT, D_MODEL), D_EXPERT ** -0.5),
        'g_final': gain(ks[21], (D_MODEL,)),
    }


def reference(x_prompt, x_sample, cache_latent, cache_kpe, state_ret, page_table,
              g_attn, w_in, g_q_a, w_q_b, g_kv_a, w_uk, w_uv, g_ret, w_o, g_ffn,
              w_router_group, w_router_expert, w_exp_gate, w_exp_up, w_exp_down, g_final):
    b, s, _ = x_prompt.shape
    db, t, _ = x_sample.shape
    past_len = page_table.shape[1] * PAGE_SIZE
    pos_p = jnp.arange(s)
    pos_s = past_len + jnp.arange(t)
    y_p, y_s = x_prompt, x_sample
    lat_p, kpe_p, ret_p, lat_s, kpe_s, ret_s = [], [], [], [], [], []
    for l in range(DEPTH):
        p = {
            'g_attn': g_attn[l], 'w_in': w_in[l], 'g_q_a': g_q_a[l], 'w_q_b': w_q_b[l],
            'g_kv_a': g_kv_a[l], 'w_uk': w_uk[l], 'w_uv': w_uv[l], 'g_ret': g_ret[l],
            'w_o': w_o[l], 'g_ffn': g_ffn[l], 'w_router_group': w_router_group[l],
            'w_router_expert': w_router_expert[l], 'w_exp_gate': w_exp_gate[l],
            'w_exp_up': w_exp_up[l], 'w_exp_down': w_exp_down[l],
        }
        s0 = jnp.zeros((b, RET_HEADS, RET_DK, RET_DV), x_prompt.dtype)
        y_p, c_p, k_p, st_p = _layer(y_p, pos_p, p, _mla_prompt, s0)
        c_past = cache_latent[l, page_table].reshape(db, past_len, KV_LORA)
        kpe_past = cache_kpe[l, page_table].reshape(db, past_len, ROPE_DIM)
        attend_s = functools.partial(_mla_sample, c_past=c_past, kpe_past=kpe_past)
        y_s, c_s, k_s, st_s = _layer(y_s, pos_s, p, attend_s, state_ret[l])
        lat_p.append(c_p); kpe_p.append(k_p); ret_p.append(st_p)
        lat_s.append(c_s); kpe_s.append(k_s); ret_s.append(st_s)
    y_prompt = _rmsnorm(y_p, g_final)
    y_sample = _rmsnorm(y_s, g_final)
    new_latent_prompt = jnp.stack(lat_p)
    new_kpe_prompt = jnp.stack(kpe_p)
    new_ret_prompt = jnp.stack(ret_p)
    new_latent_sample = jnp.stack(lat_s)
    new_kpe_sample = jnp.stack(kpe_s)
    new_ret_sample = jnp.stack(ret_s)
    return (y_prompt, y_sample, new_latent_prompt, new_kpe_prompt, new_ret_prompt,
            new_latent_sample, new_kpe_sample, new_ret_sample)
```

```python
import functools

import jax
import jax.numpy as jnp
from jax import lax
from jax.experimental import pallas as pl
from jax.experimental.pallas import tpu as pltpu

F32 = jnp.float32
BF16 = jnp.bfloat16

HEADS = 8
NOPE = 128
ROPE = 64
KVL = 512
VDIM = 128
RET_DK = 64
RET_DV = 128
PAGE = 128
N_GROUPS = 4
EXPERTS_PER_GROUP = 4
EPS = 1e-6
ROPE_BASE = 10000.0
NEG = -1e30

V7X_VMEM_BYTES = 64 * 1024 * 1024
VMEM_LIMIT = V7X_VMEM_BYTES - 8 * 1024 * 1024
LANES = 128

TM_PROJ = 512
TM_PROJ_SAMPLE = 256
TQ_ATTN = 128
PAGES_PER_STEP = 8
RET_SAMPLE_SEQS = 16
TM_MOE = 256
TM_FINAL = 512


def _mm(a, b):
    return jnp.dot(a, b, preferred_element_type=F32)


def _mm_nt(a, b):
    return lax.dot_general(a, b, (((1,), (1,)), ((), ())), preferred_element_type=F32)


def _mm_tn(a, b):
    return lax.dot_general(a, b, (((0,), (0,)), ((), ())), preferred_element_type=F32)


def _rms(x, g):
    return x * lax.rsqrt(jnp.mean(x * x, axis=-1, keepdims=True) + EPS) * g


def _silu(x):
    return x * (1.0 / (1.0 + jnp.exp(-x)))


def _rope(x, cos, sin):
    w = x.shape[-1]
    lane = lax.broadcasted_iota(jnp.int32, x.shape, 1)
    first_half = (lane % ROPE) < (ROPE // 2)
    swapped = jnp.where(first_half, pltpu.roll(x, w - ROPE // 2, 1), pltpu.roll(x, ROPE // 2, 1))
    return x * cos + swapped * sin


def _const_spec(shape):
    nd = len(shape)
    return pl.BlockSpec(shape, lambda *_: (0,) * nd, pipeline_mode=pl.Buffered(1))


def _params(*sem):
    return pltpu.CompilerParams(dimension_semantics=sem, vmem_limit_bytes=VMEM_LIMIT)


def _inproj_kernel(x_ref, cs_ref, sn_ref, gattn_ref, win_ref, gq_ref, gkv_ref, wqb_ref, wukt_ref,
                   c_ref, kpe_ref, ql_ref, qp_ref, rq_ref, rk_ref, rv_ref, rg_ref, *rest, prompt):
    h = _rms(x_ref[...], gattn_ref[...]).astype(BF16)
    cs = cs_ref[...]
    sn = sn_ref[...]
    cs4 = jnp.concatenate([cs] * 4, axis=1)
    sn4 = jnp.concatenate([sn] * 4, axis=1)

    def proj(lo, hi):
        return _mm(h, win_ref[:, lo:hi])

    c = _rms(proj(512, 1024), gkv_ref[...])
    c_ref[...] = c
    kpe128 = _rope(proj(4096, 4224), cs, sn)
    kpe_ref[...] = kpe128[:, :ROPE]

    qn = _rms(proj(0, 512), gq_ref[...]).astype(BF16)
    q = _mm(qn, wqb_ref[...])
    q_pe = _rope(q[:, HEADS * NOPE:], cs4, sn4)
    for hh in range(HEADS):
        q_nope = q[:, hh * NOPE:(hh + 1) * NOPE].astype(BF16)
        ql_ref[hh] = _mm(q_nope, wukt_ref[hh]).astype(ql_ref.dtype)
        qp_ref[hh] = q_pe[:, hh * ROPE:(hh + 1) * ROPE].astype(qp_ref.dtype)

    rq_ref[...] = (_rope(proj(1024, 1536), cs4, sn4) * (RET_DK ** -0.5)).astype(rq_ref.dtype)
    rk_ref[...] = _rope(proj(1536, 2048), cs4, sn4).astype(rk_ref.dtype)
    rv_ref[...] = proj(2048, 3072).astype(rv_ref.dtype)
    rg_ref[...] = _silu(proj(3072, 4096)).astype(rg_ref.dtype)

    if prompt:
        ckt_ref, kpt_ref, cb_ref = rest
        cb_ref[...] = c.astype(BF16)
        ckt_ref[0] = c.T.astype(BF16)
        kpt_ref[0] = kpe128.T[:ROPE].astype(BF16)


def _inproj(x2d, cs, sn, cs_index, w, *, tm, prompt):
    t_tok, d = x2d.shape
    nt = t_tok // tm
    adt = BF16 if prompt else F32
    row = lambda width: pl.BlockSpec((tm, width), lambda i: (i, 0))
    out_shape = [
        jax.ShapeDtypeStruct((t_tok, KVL), F32), jax.ShapeDtypeStruct((t_tok, ROPE), F32),
        jax.ShapeDtypeStruct((HEADS, t_tok, KVL), adt), jax.ShapeDtypeStruct((HEADS, t_tok, ROPE), adt),
        jax.ShapeDtypeStruct((t_tok, HEADS * RET_DK), adt), jax.ShapeDtypeStruct((t_tok, HEADS * RET_DK), adt),
        jax.ShapeDtypeStruct((t_tok, HEADS * RET_DV), adt), jax.ShapeDtypeStruct((t_tok, HEADS * RET_DV), adt),
    ]
    out_specs = [
        row(KVL), row(ROPE),
        pl.BlockSpec((HEADS, tm, KVL), lambda i: (0, i, 0)), pl.BlockSpec((HEADS, tm, ROPE), lambda i: (0, i, 0)),
        row(HEADS * RET_DK), row(HEADS * RET_DK), row(HEADS * RET_DV), row(HEADS * RET_DV),
    ]
    if prompt:
        out_shape += [jax.ShapeDtypeStruct((nt, KVL, tm), BF16), jax.ShapeDtypeStruct((nt, ROPE, tm), BF16),
                      jax.ShapeDtypeStruct((t_tok, KVL), BF16)]
        out_specs += [pl.BlockSpec((1, KVL, tm), lambda i: (i, 0, 0)), pl.BlockSpec((1, ROPE, tm), lambda i: (i, 0, 0)),
                      row(KVL)]
    in_specs = [
        row(d),
        pl.BlockSpec((tm, LANES), lambda i: (cs_index(i), 0)), pl.BlockSpec((tm, LANES), lambda i: (cs_index(i), 0)),
        _const_spec(w["g_attn"].shape), _const_spec(w["w_in"].shape), _const_spec(w["g_q_a"].shape),
        _const_spec(w["g_kv_a"].shape), _const_spec(w["w_qb"].shape), _const_spec(w["w_ukt"].shape),
    ]
    return pl.pallas_call(
        functools.partial(_inproj_kernel, prompt=prompt),
        grid=(nt,), in_specs=in_specs, out_specs=out_specs, out_shape=out_shape,
        compiler_params=_params("parallel"), name="inproj_prompt" if prompt else "inproj_sample",
    )(x2d, cs, sn, w["g_attn"], w["w_in"], w["g_q_a"], w["g_kv_a"], w["w_qb"], w["w_ukt"])


def _attn_prompt_kernel(ql_ref, qp_ref, ckt_ref, kpt_ref, cb_ref, wuv_ref, o_ref, m_ref, l_ref, acc_ref,
                        *, tq, tk, scale):
    qi = pl.program_id(1)
    rows = HEADS * tq
    q_l = ql_ref[...].reshape(rows, KVL)
    q_p = qp_ref[...].reshape(rows, ROPE)
    m_ref[...] = jnp.full((rows, 1), NEG, F32)
    l_ref[...] = jnp.zeros((rows, 1), F32)
    acc_ref[...] = jnp.zeros((rows, KVL), F32)
    q_pos = lax.broadcasted_iota(jnp.int32, (rows, tk), 0) % tq + qi * tq
    k_off = lax.broadcasted_iota(jnp.int32, (rows, tk), 1)

    def body(j, carry):
        s = (_mm(q_l, ckt_ref[j]) + _mm(q_p, kpt_ref[j])) * scale
        s = jnp.where(k_off + j * tk <= q_pos, s, NEG)
        m_prev = m_ref[...]
        m_new = jnp.maximum(m_prev, jnp.max(s, axis=1, keepdims=True))
        alpha = jnp.exp(m_prev - m_new)
        p = jnp.exp(s - m_new)
        l_ref[...] = alpha * l_ref[...] + jnp.sum(p, axis=1, keepdims=True)
        v = cb_ref[pl.ds(pl.multiple_of(j * tk, tk), tk), :]
        acc_ref[...] = alpha * acc_ref[...] + _mm(p.astype(BF16), v)
        m_ref[...] = m_new
        return carry

    lax.fori_loop(0, (qi * tq) // tk + 1, body, 0)
    o = acc_ref[...] / l_ref[...]
    for hh in range(HEADS):
        oh = o[hh * tq:(hh + 1) * tq].astype(BF16)
        o_ref[:, hh * VDIM:(hh + 1) * VDIM] = _mm(oh, wuv_ref[hh]).astype(o_ref.dtype)


def _attn_prompt(ql, qp, ckt, kpt, cb, wuv, *, batch, seq, tq, tk, scale):
    nq = seq // tq
    nkb = seq // tk
    rows = HEADS * tq
    return pl.pallas_call(
        functools.partial(_attn_prompt_kernel, tq=tq, tk=tk, scale=scale),
        grid=(batch, nq),
        in_specs=[
            pl.BlockSpec((HEADS, tq, KVL), lambda b, i: (0, b * nq + i, 0)),
            pl.BlockSpec((HEADS, tq, ROPE), lambda b, i: (0, b * nq + i, 0)),
            pl.BlockSpec((nkb, KVL, tk), lambda b, i: (b, 0, 0)),
            pl.BlockSpec((nkb, ROPE, tk), lambda b, i: (b, 0, 0)),
            pl.BlockSpec((seq, KVL), lambda b, i: (b, 0)),
            _const_spec(wuv.shape),
        ],
        out_specs=pl.BlockSpec((tq, HEADS * VDIM), lambda b, i: (b * nq + i, 0)),
        out_shape=jax.ShapeDtypeStruct((batch * seq, HEADS * VDIM), BF16),
        scratch_shapes=[pltpu.VMEM((rows, 1), F32), pltpu.VMEM((rows, 1), F32), pltpu.VMEM((rows, KVL), F32)],
        compiler_params=_params("parallel", "arbitrary"), name="attn_prompt",
    )(ql, qp, ckt, kpt, cb, wuv)


def _attn_sample_kernel(pt_ref, ql_ref, qp_ref, cnew_ref, knew_ref, lat_hbm, kpe_hbm, o_ref,
                        cbuf, kbuf, sem, qlb, qpb, m_ref, l_ref, acc_ref, *, pages, nblk, t_new, scale):
    b = pl.program_id(0)
    j = pl.program_id(1)
    step = b * nblk + j
    total = pl.num_programs(0) * nblk
    slot = step % 2
    rows = HEADS * t_new
    keys = pages * PAGE

    def page_copies(bb, jj, sl):
        out = []
        for p in range(pages):
            page = pt_ref[bb, jj * pages + p]
            out.append(pltpu.make_async_copy(lat_hbm.at[page], cbuf.at[sl, p], sem.at[0, sl]))
            out.append(pltpu.make_async_copy(kpe_hbm.at[page], kbuf.at[sl, p], sem.at[1, sl]))
        return out

    @pl.when(step == 0)
    def _():
        for cp in page_copies(0, 0, 0):
            cp.start()

    @pl.when(step + 1 < total)
    def _():
        nxt = step + 1
        for cp in page_copies(nxt // nblk, nxt % nblk, 1 - slot):
            cp.start()

    for cp in page_copies(b, j, slot):
        cp.wait()

    @pl.when(j == 0)
    def _():
        zpad_l = jnp.zeros((LANES - rows, KVL), F32)
        zpad_p = jnp.zeros((LANES - rows, ROPE), F32)
        qlb[...] = jnp.concatenate([ql_ref[...].reshape(rows, KVL), zpad_l], axis=0).astype(BF16)
        qpb[...] = jnp.concatenate([qp_ref[...].reshape(rows, ROPE), zpad_p], axis=0).astype(BF16)
        m_ref[...] = jnp.full((rows, 1), NEG, F32)
        l_ref[...] = jnp.zeros((rows, 1), F32)
        acc_ref[...] = jnp.zeros((rows, KVL), F32)

    def update(s, v):
        m_prev = m_ref[...]
        m_new = jnp.maximum(m_prev, jnp.max(s, axis=1, keepdims=True))
        alpha = jnp.exp(m_prev - m_new)
        p = jnp.exp(s - m_new)
        l_ref[...] = alpha * l_ref[...] + jnp.sum(p, axis=1, keepdims=True)
        acc_ref[...] = alpha * acc_ref[...] + _mm(p.astype(BF16), v)
        m_ref[...] = m_new

    kc = cbuf[slot].reshape(keys, KVL).astype(BF16)
    kp = kbuf[slot].reshape(keys, ROPE).astype(BF16)
    s_t = _mm_nt(kc, qlb[...]) + _mm_nt(kp, qpb[...])
    update(s_t.T[:rows] * scale, kc)

    @pl.when(j == nblk - 1)
    def _():
        zc = jnp.zeros((LANES - t_new, KVL), F32)
        zk = jnp.zeros((LANES - t_new, ROPE), F32)
        cn = jnp.concatenate([cnew_ref[...], zc], axis=0).astype(BF16)
        kn = jnp.concatenate([knew_ref[...], zk], axis=0).astype(BF16)
        s2 = (_mm_nt(qlb[0:rows], cn) + _mm_nt(qpb[0:rows], kn)) * scale
        q_t = lax.broadcasted_iota(jnp.int32, (rows, LANES), 0) % t_new
        k_t = lax.broadcasted_iota(jnp.int32, (rows, LANES), 1)
        update(jnp.where(k_t <= q_t, s2, NEG), cn)
        o_ref[0] = acc_ref[...] / l_ref[...]


def _attn_sample(page_table, ql, qp, c_new, kpe_new, lat_pool, kpe_pool, *, t_new, pages, scale):
    db, n_pages = page_table.shape
    nblk = n_pages // pages
    rows = HEADS * t_new
    grid_spec = pltpu.PrefetchScalarGridSpec(
        num_scalar_prefetch=1, grid=(db, nblk),
        in_specs=[
            pl.BlockSpec((HEADS, t_new, KVL), lambda b, j, pt: (0, b, 0)),
            pl.BlockSpec((HEADS, t_new, ROPE), lambda b, j, pt: (0, b, 0)),
            pl.BlockSpec((t_new, KVL), lambda b, j, pt: (b, 0)),
            pl.BlockSpec((t_new, ROPE), lambda b, j, pt: (b, 0)),
            pl.BlockSpec(memory_space=pl.ANY), pl.BlockSpec(memory_space=pl.ANY),
        ],
        out_specs=pl.BlockSpec((1, rows, KVL), lambda b, j, pt: (b, 0, 0)),
        scratch_shapes=[
            pltpu.VMEM((2, pages, PAGE, KVL), F32), pltpu.VMEM((2, pages, PAGE, ROPE), F32),
            pltpu.SemaphoreType.DMA((2, 2)),
            pltpu.VMEM((LANES, KVL), BF16), pltpu.VMEM((LANES, ROPE), BF16),
            pltpu.VMEM((rows, 1), F32), pltpu.VMEM((rows, 1), F32), pltpu.VMEM((rows, KVL), F32),
        ],
    )
    return pl.pallas_call(
        functools.partial(_attn_sample_kernel, pages=pages, nblk=nblk, t_new=t_new, scale=scale),
        grid_spec=grid_spec, out_shape=jax.ShapeDtypeStruct((db, rows, KVL), F32),
        compiler_params=_params("arbitrary", "arbitrary"), name="attn_sample",
    )(page_table, ql, qp, c_new, kpe_new, lat_pool, kpe_pool)


def _uv_sample_kernel(o_ref, wuv_ref, out_ref):
    n, t_new, _ = o_ref.shape
    out_ref[...] = _mm(o_ref[...].reshape(n * t_new, KVL).astype(BF16), wuv_ref[0]).astype(out_ref.dtype)


def _uv_sample(o_lat, wuv, *, t_new):
    db = o_lat.shape[0]
    o4 = o_lat.reshape(db, HEADS, t_new, KVL)
    return pl.pallas_call(
        _uv_sample_kernel, grid=(HEADS,),
        in_specs=[pl.BlockSpec((db, None, t_new, KVL), lambda h: (0, h, 0, 0)),
                  pl.BlockSpec((1, KVL, VDIM), lambda h: (h, 0, 0))],
        out_specs=pl.BlockSpec((db * t_new, VDIM), lambda h: (0, h)),
        out_shape=jax.ShapeDtypeStruct((db * t_new, HEADS * VDIM), BF16),
        compiler_params=_params("parallel"), name="uv_sample",
    )(o4, wuv)


def _ret_kernel(rq_ref, rk_ref, rv_ref, rg_ref, s0_ref, di_ref, dq_ref, dk_ref, ds_ref, gr_ref,
                o_ref, sf_ref, st_ref, *, seqs, t, nc):
    c = pl.program_id(1)

    @pl.when(c == 0)
    def _():
        st_ref[...] = s0_ref[...]

    q = rq_ref[...]
    k = rk_ref[...]
    v = rv_ref[...]
    outs = []
    for hh in range(HEADS):
        qh = q[:, hh * RET_DK:(hh + 1) * RET_DK]
        kh = k[:, hh * RET_DK:(hh + 1) * RET_DK]
        vh = v[:, hh * RET_DV:(hh + 1) * RET_DV]
        a = _mm_nt(qh.astype(BF16), kh.astype(BF16)) * di_ref[hh]
        o = _mm(a.astype(BF16), vh.astype(BF16))
        kd = kh.astype(F32) * dk_ref[hh]
        cross = []
        for g in range(seqs):
            sl = slice(g * t, (g + 1) * t)
            state = st_ref[g, hh]
            cross.append(_mm(qh[sl].astype(BF16), state.astype(BF16)))
            st_ref[g, hh] = state * ds_ref[hh] + _mm_tn(kd[sl].astype(BF16), vh[sl].astype(BF16))
        o = o + (cross[0] if seqs == 1 else jnp.concatenate(cross, axis=0)) * dq_ref[hh]
        outs.append(_rms(o, gr_ref[hh:hh + 1, :]))
    o_ref[...] = (rg_ref[...].astype(F32) * jnp.concatenate(outs, axis=1)).astype(o_ref.dtype)

    @pl.when(c == nc - 1)
    def _():
        sf_ref[...] = st_ref[...]


def _retention(rq, rk, rv, rg, s0, g_ret, *, seqs, t, nc):
    rows = seqs * t
    n_state = s0.shape[0]
    nb = n_state // seqs
    hh = jnp.arange(HEADS, dtype=F32)
    log_g = jnp.log1p(-jnp.exp2(-5.0 - hh))
    i = jnp.arange(t, dtype=F32)
    diff = i[:, None] - i[None, :]
    d_intra = jnp.where(diff >= 0, jnp.exp(jnp.maximum(diff, 0.0)[None] * log_g[:, None, None]), 0.0)
    d_q = jnp.exp((i[None, :] + 1.0) * log_g[:, None])
    d_k = jnp.exp((t - 1.0 - i)[None, :] * log_g[:, None])
    d_s = jnp.exp(t * log_g)
    eye = jnp.eye(seqs, dtype=F32)
    di = jnp.einsum("gk,hij->hgikj", eye, d_intra).reshape(HEADS, rows, rows)
    dq = jnp.broadcast_to(jnp.tile(d_q, (1, seqs))[:, :, None], (HEADS, rows, RET_DV))
    dk = jnp.broadcast_to(jnp.tile(d_k, (1, seqs))[:, :, None], (HEADS, rows, RET_DK))
    ds = jnp.broadcast_to(d_s[:, None, None], (HEADS, RET_DK, RET_DV))
    blk = lambda width: pl.BlockSpec((rows, width), lambda s, c: (s * nc + c, 0))
    state_spec = pl.BlockSpec((seqs, HEADS, RET_DK, RET_DV), lambda s, c: (s, 0, 0, 0))
    return pl.pallas_call(
        functools.partial(_ret_kernel, seqs=seqs, t=t, nc=nc),
        grid=(nb, nc),
        in_specs=[blk(HEADS * RET_DK), blk(HEADS * RET_DK), blk(HEADS * RET_DV), blk(HEADS * RET_DV), state_spec,
                  _const_spec(di.shape), _const_spec(dq.shape), _const_spec(dk.shape), _const_spec(ds.shape),
                  _const_spec(g_ret.shape)],
        out_specs=[blk(HEADS * RET_DV), state_spec],
        out_shape=[jax.ShapeDtypeStruct((nb * nc * rows, HEADS * RET_DV), BF16),
                   jax.ShapeDtypeStruct(s0.shape, F32)],
        scratch_shapes=[pltpu.VMEM((seqs, HEADS, RET_DK, RET_DV), F32)],
        compiler_params=_params("parallel", "arbitrary"), name="retention_t%d" % t,
    )(rq, rk, rv, rg, s0, di, dq, dk, ds, g_ret)


def _outproj_kernel(xp_ref, xs_ref, omp_ref, oms_ref, orp_ref, ors_ref, wo_ref, gffn_ref, wr_ref,
                    x2_ref, hne_ref, route_ref, *, n_prompt_tiles):
    from_prompt = pl.program_id(0) < n_prompt_tiles
    x = jnp.where(from_prompt, xp_ref[...], xs_ref[...])
    o_mla = jnp.where(from_prompt, omp_ref[...], oms_ref[...])
    o_ret = jnp.where(from_prompt, orp_ref[...], ors_ref[...])
    half = HEADS * VDIM
    x2 = x + _mm(o_mla, wo_ref[0:half, :]) + _mm(o_ret, wo_ref[half:, :])
    x2_ref[...] = x2
    hn = _rms(x2, gffn_ref[...])
    d = hn.shape[1]
    hne_ref[:, 0:d] = hn
    logits = _mm(hn.astype(BF16), wr_ref[...])

    def col(kk):
        return logits[:, kk:kk + 1]

    best = col(0)
    grp = jnp.zeros(best.shape, jnp.int32)
    for kk in range(1, N_GROUPS):
        upd = col(kk) > best
        grp = jnp.where(upd, kk, grp)
        best = jnp.where(upd, col(kk), best)
    den = jnp.exp(col(0) - best)
    for kk in range(1, N_GROUPS):
        den = den + jnp.exp(col(kk) - best)
    gate = 1.0 / den

    le = []
    for jj in range(EXPERTS_PER_GROUP):
        sel = col(N_GROUPS + (N_GROUPS - 1) * EXPERTS_PER_GROUP + jj)
        for gg in range(N_GROUPS - 2, -1, -1):
            sel = jnp.where(grp == gg, col(N_GROUPS + gg * EXPERTS_PER_GROUP + jj), sel)
        le.append(sel)
    mx = jnp.maximum(jnp.maximum(le[0], le[1]), jnp.maximum(le[2], le[3]))
    ex = [jnp.exp(v - mx) for v in le]
    chosen = []
    for jj in range(EXPERTS_PER_GROUP):
        rank = jnp.zeros(best.shape, jnp.int32)
        for ii in range(EXPERTS_PER_GROUP):
            if ii == jj:
                continue
            ahead = (ex[ii] > ex[jj]) | ((ex[ii] == ex[jj]) & (ii < jj))
            rank = rank + ahead.astype(jnp.int32)
        chosen.append(rank < 2)
    den2 = jnp.zeros(best.shape, F32)
    for jj in range(EXPERTS_PER_GROUP):
        den2 = den2 + jnp.where(chosen[jj], ex[jj], 0.0)
    lane = lax.broadcasted_iota(jnp.int32, logits.shape, 1)
    route = jnp.where(lane == EXPERTS_PER_GROUP, grp.astype(F32), 0.0)
    for jj in range(EXPERTS_PER_GROUP):
        comb = jnp.where(chosen[jj], ex[jj] / den2 * gate, 0.0)
        route = jnp.where(lane == jj, comb, route)
    hne_ref[:, d:] = route
    route_ref[...] = route


def _outproj(x_p, x_s, om_p, om_s, or_p, or_s, w, *, tm):
    tp, d = x_p.shape
    ts = x_s.shape[0]
    npt = tp // tm
    nst = ts // tm
    p_idx = lambda i: (jnp.minimum(i, npt - 1), 0)
    s_idx = lambda i: (jnp.maximum(i - npt, 0), 0)
    mix = om_p.shape[1]
    return pl.pallas_call(
        functools.partial(_outproj_kernel, n_prompt_tiles=npt),
        grid=(npt + nst,),
        in_specs=[pl.BlockSpec((tm, d), p_idx), pl.BlockSpec((tm, d), s_idx),
                  pl.BlockSpec((tm, mix), p_idx), pl.BlockSpec((tm, mix), s_idx),
                  pl.BlockSpec((tm, mix), p_idx), pl.BlockSpec((tm, mix), s_idx),
                  _const_spec(w["w_o"].shape), _const_spec(w["g_ffn"].shape), _const_spec(w["w_r"].shape)],
        out_specs=[pl.BlockSpec((tm, d), lambda i: (i, 0)), pl.BlockSpec((tm, d + LANES), lambda i: (i, 0)),
                   pl.BlockSpec((tm, LANES), lambda i: (i, 0))],
        out_shape=[jax.ShapeDtypeStruct((tp + ts, d), F32), jax.ShapeDtypeStruct((tp + ts, d + LANES), F32),
                   jax.ShapeDtypeStruct((tp + ts, LANES), F32)],
        compiler_params=_params("parallel"), name="outproj_router",
    )(x_p, x_s, om_p, om_s, or_p, or_s, w["w_o"], w["g_ffn"], w["w_r"])


def _gather_rows(idx_ref, base, n, src_hbm, buf, sem):
    def issue(r, carry):
        pltpu.make_async_copy(src_hbm.at[pl.ds(idx_ref[base + r], 1)], buf.at[pl.ds(r, 1)], sem).start()
        return carry
    lax.fori_loop(0, n, issue, 0)
    pltpu.make_async_copy(src_hbm.at[pl.ds(0, n)], buf, sem).wait()


def _moe_kernel(src_ref, tg_ref, tv_ref, hne_hbm, wg_ref, wu_ref, wd_ref, out_ref, buf, sem, *, tmm, d):
    i = pl.program_id(0)

    @pl.when(tv_ref[i] == 1)
    def _():
        _gather_rows(src_ref, i * tmm, tmm, hne_hbm, buf, sem.at[0])
        hb = buf[:, 0:d].astype(BF16)
        acc = jnp.zeros((tmm, d), F32)
        for jj in range(EXPERTS_PER_GROUP):
            a = _silu(_mm(hb, wg_ref[jj])) * _mm(hb, wu_ref[jj])
            a = a * buf[:, d + jj:d + jj + 1]
            acc = acc + _mm(a.astype(BF16), wd_ref[jj])
        out_ref[...] = acc

    @pl.when(tv_ref[i] == 0)
    def _():
        out_ref[...] = jnp.zeros(out_ref.shape, F32)


def _moe(src, tile_group, tile_valid, hne, wg, wu, wd, *, tmm):
    n_rows = src.shape[0]
    nt = n_rows // tmm
    d = hne.shape[1] - LANES
    f = wg.shape[2]
    wspec = lambda shape: pl.BlockSpec(shape, lambda i, s, tg, tv: (tg[i], 0, 0), pipeline_mode=pl.Buffered(1))
    grid_spec = pltpu.PrefetchScalarGridSpec(
        num_scalar_prefetch=3, grid=(nt,),
        in_specs=[pl.BlockSpec(memory_space=pl.ANY),
                  wspec((EXPERTS_PER_GROUP, d, f)), wspec((EXPERTS_PER_GROUP, d, f)), wspec((EXPERTS_PER_GROUP, f, d))],
        out_specs=pl.BlockSpec((tmm, d), lambda i, s, tg, tv: (i, 0)),
        scratch_shapes=[pltpu.VMEM((tmm, d + LANES), F32), pltpu.SemaphoreType.DMA((1,))],
    )
    return pl.pallas_call(
        functools.partial(_moe_kernel, tmm=tmm, d=d),
        grid_spec=grid_spec, out_shape=jax.ShapeDtypeStruct((n_rows, d), F32),
        compiler_params=_params("arbitrary"), name="moe_grouped",
    )(src, tile_group, tile_valid, hne, wg, wu, wd)


def _final_kernel(pos_ref, x2_ref, moe_hbm, gfin_ref, y_ref, buf, sem, *, tm):
    _gather_rows(pos_ref, pl.program_id(0) * tm, tm, moe_hbm, buf, sem.at[0])
    y_ref[...] = _rms(x2_ref[...] + buf[...], gfin_ref[...])


def _final(pos, x2, moe_sorted, g_final, *, tm, tile0):
    n = pos.shape[0]
    d = x2.shape[1]
    grid_spec = pltpu.PrefetchScalarGridSpec(
        num_scalar_prefetch=1, grid=(n // tm,),
        in_specs=[pl.BlockSpec((tm, d), lambda i, p: (i + tile0, 0)),
                  pl.BlockSpec(memory_space=pl.ANY),
                  pl.BlockSpec(g_final.shape, lambda i, p: (0, 0))],
        out_specs=pl.BlockSpec((tm, d), lambda i, p: (i, 0)),
        scratch_shapes=[pltpu.VMEM((tm, d), F32), pltpu.SemaphoreType.DMA((1,))],
    )
    return pl.pallas_call(
        functools.partial(_final_kernel, tm=tm),
        grid_spec=grid_spec, out_shape=jax.ShapeDtypeStruct((n, d), F32),
        compiler_params=_params("arbitrary"), name="final_norm",
    )(pos, x2, moe_sorted, g_final)


def _rope_tables(pos):
    inv = 1.0 / (ROPE_BASE ** (jnp.arange(0, ROPE, 2, dtype=F32) / ROPE))
    ang = pos.astype(F32)[:, None] * inv[None, :]
    cos = jnp.cos(ang)
    sin = jnp.sin(ang)
    return jnp.concatenate([cos] * 4, axis=1), jnp.concatenate([-sin, sin] * 2, axis=1)


def _sort_by_group(group, tmm):
    n = group.shape[0]
    onehot = (group[:, None] == jnp.arange(N_GROUPS, dtype=jnp.int32)[None, :]).astype(jnp.int32)
    csum = jnp.cumsum(onehot, axis=0)
    counts = csum[-1]
    rank = jnp.take_along_axis(csum, group[:, None], axis=1)[:, 0] - 1
    padded = ((counts + tmm - 1) // tmm) * tmm
    ends = jnp.cumsum(padded)
    pos = (ends - padded)[group] + rank
    n_tiles = n // tmm + N_GROUPS
    src = jnp.zeros((n_tiles * tmm,), jnp.int32).at[pos].set(jnp.arange(n, dtype=jnp.int32))
    starts = jnp.arange(n_tiles, dtype=jnp.int32) * tmm
    tg = jnp.sum((starts[:, None] >= ends[None, :]).astype(jnp.int32), axis=1)
    valid = (tg < N_GROUPS).astype(jnp.int32)
    return pos.astype(jnp.int32), src, jnp.minimum(tg, N_GROUPS - 1).astype(jnp.int32), valid


def kernel(x_prompt, x_sample, cache_latent, cache_kpe, state_ret, page_table, g_attn, w_in, g_q_a, w_q_b,
           g_kv_a, w_uk, w_uv, g_ret, w_o, g_ffn, w_router_group, w_router_expert, w_exp_gate, w_exp_up,
           w_exp_down, g_final):
    batch, seq, d = x_prompt.shape
    db, t_new, _ = x_sample.shape
    depth = w_in.shape[0]
    assert depth == 1, "single-layer step"
    n_pool = cache_latent.shape[1]
    past_len = page_table.shape[1] * PAGE
    tp, ts = batch * seq, db * t_new
    tm = min(TM_PROJ, seq, ts)
    assert seq % tm == 0 and ts % tm == 0 and tm % TQ_ATTN == 0 and (tp + ts) % TM_MOE == 0
    pages = min(PAGES_PER_STEP, page_table.shape[1])
    assert page_table.shape[1] % pages == 0
    seqs = min(RET_SAMPLE_SEQS, db)
    assert db % seqs == 0 and seq % PAGE == 0
    scale = (NOPE + ROPE) ** -0.5

    wi = w_in[0]
    q_a, kv_a, k_pe, rq, rk, rv, rg = (wi[:, 0:512], wi[:, 512:1024], wi[:, 1024:1088], wi[:, 1088:1600],
                                       wi[:, 1600:2112], wi[:, 2112:3136], wi[:, 3136:4160])
    w_in_r = jnp.concatenate([q_a, kv_a, rq, rk, rv, rg, k_pe, jnp.zeros((d, LANES - ROPE), F32)], axis=1)
    wqb = w_q_b[0]
    w = {
        "g_attn": g_attn[0][None, :], "w_in": w_in_r.astype(BF16),
        "g_q_a": g_q_a[0][None, :], "g_kv_a": g_kv_a[0][None, :],
        "w_qb": jnp.concatenate([wqb[:, :, :NOPE].reshape(KVL, HEADS * NOPE),
                                 wqb[:, :, NOPE:].reshape(KVL, HEADS * ROPE)], axis=1).astype(BF16),
        "w_ukt": jnp.transpose(w_uk[0], (1, 2, 0)).astype(BF16),
        "w_o": w_o[0].astype(BF16), "g_ffn": g_ffn[0][None, :],
        "w_r": jnp.concatenate([w_router_group[0], w_router_expert[0],
                                jnp.zeros((d, LANES - N_GROUPS - N_GROUPS * EXPERTS_PER_GROUP), F32)],
                               axis=1).astype(BF16),
    }
    wuv = jnp.transpose(w_uv[0], (1, 0, 2)).astype(BF16)
    wg, wu, wd = w_exp_gate[0].astype(BF16), w_exp_up[0].astype(BF16), w_exp_down[0].astype(BF16)

    cs_p, sn_p = _rope_tables(jnp.arange(seq))
    cs_s, sn_s = _rope_tables(past_len + jnp.arange(t_new))
    cs_s, sn_s = jnp.tile(cs_s, (db, 1)), jnp.tile(sn_s, (db, 1))
    tiles_per_seq = seq // tm

    (c_p, kpe_p, ql_p, qp_p, rq_p, rk_p, rv_p, rg_p, ckt, kpt, cb) = _inproj(
        x_prompt.reshape(tp, d), cs_p, sn_p, lambda i: i % tiles_per_seq, w, tm=tm, prompt=True)
    om_p = _attn_prompt(ql_p, qp_p, ckt, kpt, cb, wuv, batch=batch, seq=seq, tq=TQ_ATTN, tk=tm, scale=scale)
    or_p, st_p = _retention(rq_p, rk_p, rv_p, rg_p, jnp.zeros((batch, HEADS, RET_DK, RET_DV), F32), g_ret[0],
                            seqs=1, t=PAGE, nc=seq // PAGE)

    (c_s, kpe_s, ql_s, qp_s, rq_s, rk_s, rv_s, rg_s) = _inproj(
        x_sample.reshape(ts, d), cs_s, sn_s, lambda i: i, w, tm=min(TM_PROJ_SAMPLE, ts), prompt=False)
    o_lat_s = _attn_sample(page_table, ql_s, qp_s, c_s, kpe_s,
                           cache_latent.reshape(n_pool, PAGE, KVL), cache_kpe.reshape(n_pool, PAGE, ROPE),
                           t_new=t_new, pages=pages, scale=scale)
    om_s = _uv_sample(o_lat_s, wuv, t_new=t_new)
    or_s, st_s = _retention(rq_s, rk_s, rv_s, rg_s, state_ret[0], g_ret[0], seqs=seqs, t=t_new, nc=1)

    x2, hne, route = _outproj(x_prompt.reshape(tp, d), x_sample.reshape(ts, d), om_p, om_s, or_p, or_s, w, tm=tm)
    pos, src, tile_group, tile_valid = _sort_by_group(route[:, EXPERTS_PER_GROUP].astype(jnp.int32), TM_MOE)
    moe_sorted = _moe(src, tile_group, tile_valid, hne, wg, wu, wd, tmm=TM_MOE)
    gfin = g_final[None, :]
    tmf = min(TM_FINAL, ts)
    y_p = _final(pos[:tp], x2, moe_sorted, gfin, tm=tmf, tile0=0)
    y_s = _final(pos[tp:], x2, moe_sorted, gfin, tm=tmf, tile0=tp // tmf)

    return (y_p.reshape(batch, seq, d), y_s.reshape(db, t_new, d),
            c_p.reshape(1, batch, seq, KVL), kpe_p.reshape(1, batch, seq, ROPE), st_p[None],
            c_s.reshape(1, db, t_new, KVL), kpe_s.reshape(1, db, t_new, ROPE), st_s[None])
```

```python
import functools

import jax
import jax.numpy as jnp
from jax import lax
from jax.experimental import pallas as pl
from jax.experimental.pallas import tpu as pltpu

F32 = jnp.float32
BF16 = jnp.bfloat16

HEADS = 8
NOPE = 128
ROPE = 64
KVL = 512
VDIM = 128
RET_DK = 64
RET_DV = 128
PAGE = 128
N_GROUPS = 4
EXPERTS_PER_GROUP = 4
EPS = 1e-6
ROPE_BASE = 10000.0
NEG = -1e30
LOG2E = 1.4426950408889634
Q_SCALE = (NOPE + ROPE) ** -0.5 * LOG2E
QK_W = KVL + 128

V7X_VMEM_BYTES = 64 * 1024 * 1024
VMEM_LIMIT = V7X_VMEM_BYTES - 4 * 1024 * 1024
LANES = 128

TM_PROJ = 512
TM_PROJ_SAMPLE = 256
TQ_ATTN = 128
PAGES_PER_STEP = 32
PAGES_PER_CHAIN = 32
RET_SAMPLE_SEQS = 16
TM_MOE = 256
TM_FINAL = 512


def _mm(a, b):
    return jnp.dot(a, b, preferred_element_type=F32)


def _mm_nt(a, b):
    return lax.dot_general(a, b, (((1,), (1,)), ((), ())), preferred_element_type=F32)


def _mm_tn(a, b):
    return lax.dot_general(a, b, (((0,), (0,)), ((), ())), preferred_element_type=F32)


def _rms(x, g):
    return x * lax.rsqrt(jnp.mean(x * x, axis=-1, keepdims=True) + EPS) * g


def _silu(x):
    return x * (1.0 / (1.0 + jnp.exp(-x)))


def _rope(x, cos, sin):
    w = x.shape[-1]
    lane = lax.broadcasted_iota(jnp.int32, x.shape, 1)
    first_half = (lane % ROPE) < (ROPE // 2)
    swapped = jnp.where(first_half, pltpu.roll(x, w - ROPE // 2, 1), pltpu.roll(x, ROPE // 2, 1))
    return x * cos + swapped * sin


def _const_spec(shape):
    nd = len(shape)
    return pl.BlockSpec(shape, lambda *_: (0,) * nd, pipeline_mode=pl.Buffered(1))


def _params(*sem):
    return pltpu.CompilerParams(dimension_semantics=sem, vmem_limit_bytes=VMEM_LIMIT)


def _inproj_kernel(x_ref, cs_ref, sn_ref, gattn_ref, win_ref, gq_ref, gkv_ref, wqb_ref, wukt_ref,
                   c_ref, kpe_ref, q_ref, rq_ref, rk_ref, rv_ref, rg_ref, *rest, prompt):
    h = _rms(x_ref[...], gattn_ref[...]).astype(BF16)
    cs = cs_ref[...]
    sn = sn_ref[...]
    cs4 = jnp.concatenate([cs] * 4, axis=1)
    sn4 = jnp.concatenate([sn] * 4, axis=1)
    cs8 = jnp.concatenate([cs4] * 2, axis=1)
    sn8 = jnp.concatenate([sn4] * 2, axis=1)

    def proj(lo, hi):
        return _mm(h, win_ref[:, lo:hi])

    c = _rms(proj(512, 1024), gkv_ref[...])
    c_ref[...] = c
    kpe128 = _rope(proj(4096, 4224), cs, sn)
    kpe_ref[...] = kpe128[:, :ROPE]

    qn = _rms(proj(0, 512), gq_ref[...]).astype(BF16)
    q = _mm(qn, wqb_ref[...])
    q_pe = _rope(q[:, HEADS * NOPE:], cs8, sn8) * Q_SCALE
    for hh in range(HEADS):
        q_nope = q[:, hh * NOPE:(hh + 1) * NOPE].astype(BF16)
        q_lat = _mm(q_nope, wukt_ref[hh]) * Q_SCALE
        q_ref[hh] = jnp.concatenate([q_lat, q_pe[:, hh * LANES:(hh + 1) * LANES]], axis=1).astype(q_ref.dtype)

    rq_ref[...] = (_rope(proj(1024, 1536), cs4, sn4) * (RET_DK ** -0.5)).astype(rq_ref.dtype)
    rk_ref[...] = _rope(proj(1536, 2048), cs4, sn4).astype(rk_ref.dtype)
    rv_ref[...] = proj(2048, 3072).astype(rv_ref.dtype)
    rg_ref[...] = _silu(proj(3072, 4096)).astype(rg_ref.dtype)

    if prompt:
        kt_ref, cb_ref = rest
        cb_ref[...] = c.astype(BF16)
        kt_ref[0] = jnp.concatenate([c.T, kpe128.T], axis=0).astype(BF16)


def _inproj(x2d, cs, sn, cs_index, w, *, tm, prompt):
    t_tok, d = x2d.shape
    nt = t_tok // tm
    adt = BF16 if prompt else F32
    row = lambda width: pl.BlockSpec((tm, width), lambda i: (i, 0))
    out_shape = [
        jax.ShapeDtypeStruct((t_tok, KVL), F32), jax.ShapeDtypeStruct((t_tok, ROPE), F32),
        jax.ShapeDtypeStruct((HEADS, t_tok, QK_W), adt),
        jax.ShapeDtypeStruct((t_tok, HEADS * RET_DK), adt), jax.ShapeDtypeStruct((t_tok, HEADS * RET_DK), adt),
        jax.ShapeDtypeStruct((t_tok, HEADS * RET_DV), adt), jax.ShapeDtypeStruct((t_tok, HEADS * RET_DV), adt),
    ]
    out_specs = [
        row(KVL), row(ROPE),
        pl.BlockSpec((HEADS, tm, QK_W), lambda i: (0, i, 0)),
        row(HEADS * RET_DK), row(HEADS * RET_DK), row(HEADS * RET_DV), row(HEADS * RET_DV),
    ]
    if prompt:
        out_shape += [jax.ShapeDtypeStruct((nt, QK_W, tm), BF16), jax.ShapeDtypeStruct((t_tok, KVL), BF16)]
        out_specs += [pl.BlockSpec((1, QK_W, tm), lambda i: (i, 0, 0)), row(KVL)]
    in_specs = [
        row(d),
        pl.BlockSpec((tm, LANES), lambda i: (cs_index(i), 0)), pl.BlockSpec((tm, LANES), lambda i: (cs_index(i), 0)),
        _const_spec(w["g_attn"].shape), _const_spec(w["w_in"].shape), _const_spec(w["g_q_a"].shape),
        _const_spec(w["g_kv_a"].shape), _const_spec(w["w_qb"].shape), _const_spec(w["w_ukt"].shape),
    ]
    return pl.pallas_call(
        functools.partial(_inproj_kernel, prompt=prompt),
        grid=(nt,), in_specs=in_specs, out_specs=out_specs, out_shape=out_shape,
        compiler_params=_params("parallel"), name="inproj_prompt" if prompt else "inproj_sample",
    )(x2d, cs, sn, w["g_attn"], w["w_in"], w["g_q_a"], w["g_kv_a"], w["w_qb"], w["w_ukt"])


def _attn_prompt_kernel(q_ref, kt_ref, cb_ref, wuv_ref, o_ref, m_ref, l_ref, acc_ref, sa_ref, sb_ref, *, tq, tk):
    qi = pl.program_id(1)
    rows = HEADS * tq
    q = q_ref[...].reshape(rows, QK_W)
    m_ref[...] = jnp.full((rows, 1), NEG, F32)
    l_ref[...] = jnp.zeros((rows, 1), F32)
    acc_ref[...] = jnp.zeros((rows, KVL), F32)

    def scores(j, s_ref):
        s_ref[...] = _mm(q, kt_ref[j])

    def consume(s_ref, j, masked):
        s = s_ref[...]
        if masked:
            q_pos = lax.broadcasted_iota(jnp.int32, (rows, tk), 0) % tq + qi * tq
            k_pos = lax.broadcasted_iota(jnp.int32, (rows, tk), 1) + j * tk
            s = jnp.where(k_pos <= q_pos, s, NEG)
        m_prev = m_ref[...]
        m_new = jnp.maximum(m_prev, jnp.max(s, axis=1, keepdims=True))
        alpha = jnp.exp2(m_prev - m_new)
        p = jnp.exp2(s - m_new)
        l_ref[...] = alpha * l_ref[...] + jnp.sum(p, axis=1, keepdims=True)
        v = cb_ref[pl.ds(pl.multiple_of(j * tk, tk), tk), :]
        acc_ref[...] = alpha * acc_ref[...] + _mm(p.astype(BF16), v)
        m_ref[...] = m_new

    last = (qi * tq) // tk
    scores(0, sa_ref)

    def pair(i, carry):
        j = 2 * i
        scores(j + 1, sb_ref)
        consume(sa_ref, j, False)
        scores(j + 2, sa_ref)
        consume(sb_ref, j + 1, False)
        return carry

    lax.fori_loop(0, last // 2, pair, 0)

    @pl.when(last % 2 == 0)
    def _():
        consume(sa_ref, last, True)

    @pl.when(last % 2 == 1)
    def _():
        scores(last, sb_ref)
        consume(sa_ref, last - 1, False)
        consume(sb_ref, last, True)

    o = acc_ref[...] / l_ref[...]
    for hh in range(HEADS):
        oh = o[hh * tq:(hh + 1) * tq].astype(BF16)
        o_ref[:, hh * VDIM:(hh + 1) * VDIM] = _mm(oh, wuv_ref[hh]).astype(o_ref.dtype)


def _attn_prompt(q, kt, cb, wuv, *, batch, seq, tq, tk):
    nq = seq // tq
    nkb = seq // tk
    rows = HEADS * tq
    return pl.pallas_call(
        functools.partial(_attn_prompt_kernel, tq=tq, tk=tk),
        grid=(batch, nq),
        in_specs=[
            pl.BlockSpec((HEADS, tq, QK_W), lambda b, i: (0, b * nq + i, 0)),
            pl.BlockSpec((nkb, QK_W, tk), lambda b, i: (b, 0, 0)),
            pl.BlockSpec((seq, KVL), lambda b, i: (b, 0)),
            _const_spec(wuv.shape),
        ],
        out_specs=pl.BlockSpec((tq, HEADS * VDIM), lambda b, i: (b * nq + i, 0)),
        out_shape=jax.ShapeDtypeStruct((batch * seq, HEADS * VDIM), BF16),
        scratch_shapes=[pltpu.VMEM((rows, 1), F32), pltpu.VMEM((rows, 1), F32), pltpu.VMEM((rows, KVL), F32),
                        pltpu.VMEM((rows, tk), F32), pltpu.VMEM((rows, tk), F32)],
        compiler_params=_params("parallel", "arbitrary"), name="attn_prompt",
    )(q, kt, cb, wuv)


def _attn_sample_kernel(pt_ref, q_ref, cnew_ref, knew_ref, lat_hbm, kpet_hbm, o_ref,
                        cbuf, kbuf, sem, qt_ref, qpe_ref, m_ref, l_ref, acc_ref, *, pages, sub, nblk, t_new):
    b = pl.program_id(0)
    j = pl.program_id(1)
    step = b * nblk + j
    total = pl.num_programs(0) * nblk
    slot = step % 2
    rows = HEADS * t_new

    def start_pages(bb, jj, sl):
        for p in range(pages):
            page = pt_ref[bb, jj * pages + p]
            pltpu.make_async_copy(lat_hbm.at[page], cbuf.at[sl, p], sem.at[0, sl]).start()
            pltpu.make_async_copy(kpet_hbm.at[page], kbuf.at[sl, p], sem.at[1, sl]).start()

    def wait_pages(sl):
        pltpu.make_async_copy(lat_hbm.at[pl.ds(0, pages)], cbuf.at[sl], sem.at[0, sl]).wait()
        pltpu.make_async_copy(kpet_hbm.at[pl.ds(0, pages)], kbuf.at[sl], sem.at[1, sl]).wait()

    @pl.when(step == 0)
    def _():
        start_pages(0, 0, 0)

    @pl.when(step + 1 < total)
    def _():
        nxt = step + 1
        start_pages(nxt // nblk, nxt % nblk, 1 - slot)

    wait_pages(slot)

    @pl.when(j == 0)
    def _():
        q = q_ref[...].reshape(rows, QK_W)
        zpad = jnp.zeros((LANES - rows, QK_W), F32)
        qt_ref[...] = jnp.concatenate([q, zpad], axis=0).T.astype(BF16)
        qpe_ref[...] = q[:, KVL:KVL + ROPE].astype(BF16)
        m_ref[...] = jnp.full((rows, 1), NEG, F32)
        l_ref[...] = jnp.zeros((rows, 1), F32)
        acc_ref[...] = jnp.zeros((rows, KVL), F32)

    def partial_softmax(s, v):
        m_c = jnp.max(s, axis=1, keepdims=True)
        p = jnp.exp2(s - m_c)
        return m_c, jnp.sum(p, axis=1, keepdims=True), _mm(p.astype(BF16), v)

    def merge(parts):
        m_prev = m_ref[...]
        m_new = m_prev
        for m_c, _, _ in parts:
            m_new = jnp.maximum(m_new, m_c)
        alpha = jnp.exp2(m_prev - m_new)
        l_new = alpha * l_ref[...]
        acc = alpha * acc_ref[...]
        for m_c, l_c, a_c in parts:
            w_c = jnp.exp2(m_c - m_new)
            l_new = l_new + w_c * l_c
            acc = acc + w_c * a_c
        m_ref[...] = m_new
        l_ref[...] = l_new
        acc_ref[...] = acc

    q_lat_t = qt_ref[0:KVL, :]
    q_pe = qpe_ref[...]
    parts = []
    for c0 in range(0, pages, sub):
        kc = cbuf[slot, c0:c0 + sub].reshape(sub * PAGE, KVL).astype(BF16)
        kpt = jnp.concatenate([kbuf[slot, c0 + p] for p in range(sub)], axis=1).astype(BF16)
        parts.append(partial_softmax(_mm(kc, q_lat_t).T[:rows] + _mm(q_pe, kpt), kc))
    merge(parts)

    @pl.when(j == nblk - 1)
    def _():
        zc = jnp.zeros((LANES - t_new, KVL), F32)
        zk = jnp.zeros((LANES - t_new, ROPE), F32)
        cn = jnp.concatenate([cnew_ref[...], zc], axis=0).astype(BF16)
        kn = jnp.concatenate([knew_ref[...], zk], axis=0).astype(BF16)
        s2 = _mm(cn, q_lat_t).T[:rows] + _mm_nt(q_pe, kn)
        q_t = lax.broadcasted_iota(jnp.int32, (rows, LANES), 0) % t_new
        k_t = lax.broadcasted_iota(jnp.int32, (rows, LANES), 1)
        merge([partial_softmax(jnp.where(k_t <= q_t, s2, NEG), cn)])
        o_ref[0] = acc_ref[...] / l_ref[...]


def _attn_sample(page_table, q, c_new, kpe_new, lat_pool, kpet_pool, *, t_new, pages, sub):
    db, n_pages = page_table.shape
    nblk = n_pages // pages
    rows = HEADS * t_new
    grid_spec = pltpu.PrefetchScalarGridSpec(
        num_scalar_prefetch=1, grid=(db, nblk),
        in_specs=[
            pl.BlockSpec((HEADS, t_new, QK_W), lambda b, j, pt: (0, b, 0)),
            pl.BlockSpec((t_new, KVL), lambda b, j, pt: (b, 0)),
            pl.BlockSpec((t_new, ROPE), lambda b, j, pt: (b, 0)),
            pl.BlockSpec(memory_space=pl.ANY), pl.BlockSpec(memory_space=pl.ANY),
        ],
        out_specs=pl.BlockSpec((1, rows, KVL), lambda b, j, pt: (b, 0, 0)),
        scratch_shapes=[
            pltpu.VMEM((2, pages, PAGE, KVL), F32), pltpu.VMEM((2, pages, ROPE, PAGE), F32),
            pltpu.SemaphoreType.DMA((2, 2)),
            pltpu.VMEM((QK_W, LANES), BF16), pltpu.VMEM((rows, ROPE), BF16),
            pltpu.VMEM((rows, 1), F32), pltpu.VMEM((rows, 1), F32), pltpu.VMEM((rows, KVL), F32),
        ],
    )
    return pl.pallas_call(
        functools.partial(_attn_sample_kernel, pages=pages, sub=sub, nblk=nblk, t_new=t_new),
        grid_spec=grid_spec, out_shape=jax.ShapeDtypeStruct((db, rows, KVL), F32),
        compiler_params=_params("arbitrary", "arbitrary"), name="attn_sample",
    )(page_table, q, c_new, kpe_new, lat_pool, kpet_pool)


def _uv_sample_kernel(o_ref, wuv_ref, out_ref):
    n, t_new, _ = o_ref.shape
    out_ref[...] = _mm(o_ref[...].reshape(n * t_new, KVL).astype(BF16), wuv_ref[0]).astype(out_ref.dtype)


def _uv_sample(o_lat, wuv, *, t_new):
    db = o_lat.shape[0]
    o4 = o_lat.reshape(db, HEADS, t_new, KVL)
    return pl.pallas_call(
        _uv_sample_kernel, grid=(HEADS,),
        in_specs=[pl.BlockSpec((db, None, t_new, KVL), lambda h: (0, h, 0, 0)),
                  pl.BlockSpec((1, KVL, VDIM), lambda h: (h, 0, 0))],
        out_specs=pl.BlockSpec((db * t_new, VDIM), lambda h: (0, h)),
        out_shape=jax.ShapeDtypeStruct((db * t_new, HEADS * VDIM), BF16),
        compiler_params=_params("parallel"), name="uv_sample",
    )(o4, wuv)


def _ret_kernel(rq_ref, rk_ref, rv_ref, rg_ref, s0_ref, di_ref, dq_ref, dk_ref, ds_ref, gr_ref,
                o_ref, sf_ref, st_ref, *, seqs, t, nc):
    c = pl.program_id(1)

    @pl.when(c == 0)
    def _():
        st_ref[...] = s0_ref[...]

    q = rq_ref[...]
    k = rk_ref[...]
    v = rv_ref[...]
    outs = []
    for hh in range(HEADS):
        qh = q[:, hh * RET_DK:(hh + 1) * RET_DK]
        kh = k[:, hh * RET_DK:(hh + 1) * RET_DK]
        vh = v[:, hh * RET_DV:(hh + 1) * RET_DV]
        a = _mm_nt(qh.astype(BF16), kh.astype(BF16)) * di_ref[hh]
        o = _mm(a.astype(BF16), vh.astype(BF16))
        kd = kh.astype(F32) * dk_ref[hh]
        cross = []
        for g in range(seqs):
            sl = slice(g * t, (g + 1) * t)
            state = st_ref[g, hh]
            cross.append(_mm(qh[sl].astype(BF16), state.astype(BF16)))
            st_ref[g, hh] = state * ds_ref[hh] + _mm_tn(kd[sl].astype(BF16), vh[sl].astype(BF16))
        o = o + (cross[0] if seqs == 1 else jnp.concatenate(cross, axis=0)) * dq_ref[hh]
        outs.append(_rms(o, gr_ref[hh:hh + 1, :]))
    o_ref[...] = (rg_ref[...].astype(F32) * jnp.concatenate(outs, axis=1)).astype(o_ref.dtype)

    @pl.when(c == nc - 1)
    def _():
        sf_ref[...] = st_ref[...]


def _retention(rq, rk, rv, rg, s0, g_ret, *, seqs, t, nc):
    rows = seqs * t
    n_state = s0.shape[0]
    nb = n_state // seqs
    hh = jnp.arange(HEADS, dtype=F32)
    log_g = jnp.log1p(-jnp.exp2(-5.0 - hh))
    i = jnp.arange(t, dtype=F32)
    diff = i[:, None] - i[None, :]
    d_intra = jnp.where(diff >= 0, jnp.exp(jnp.maximum(diff, 0.0)[None] * log_g[:, None, None]), 0.0)
    d_q = jnp.exp((i[None, :] + 1.0) * log_g[:, None])
    d_k = jnp.exp((t - 1.0 - i)[None, :] * log_g[:, None])
    d_s = jnp.exp(t * log_g)
    eye = jnp.eye(seqs, dtype=F32)
    di = jnp.einsum("gk,hij->hgikj", eye, d_intra).reshape(HEADS, rows, rows)
    dq = jnp.broadcast_to(jnp.tile(d_q, (1, seqs))[:, :, None], (HEADS, rows, RET_DV))
    dk = jnp.broadcast_to(jnp.tile(d_k, (1, seqs))[:, :, None], (HEADS, rows, RET_DK))
    ds = jnp.broadcast_to(d_s[:, None, None], (HEADS, RET_DK, RET_DV))
    blk = lambda width: pl.BlockSpec((rows, width), lambda s, c: (s * nc + c, 0))
    state_spec = pl.BlockSpec((seqs, HEADS, RET_DK, RET_DV), lambda s, c: (s, 0, 0, 0))
    return pl.pallas_call(
        functools.partial(_ret_kernel, seqs=seqs, t=t, nc=nc),
        grid=(nb, nc),
        in_specs=[blk(HEADS * RET_DK), blk(HEADS * RET_DK), blk(HEADS * RET_DV), blk(HEADS * RET_DV), state_spec,
                  _const_spec(di.shape), _const_spec(dq.shape), _const_spec(dk.shape), _const_spec(ds.shape),
                  _const_spec(g_ret.shape)],
        out_specs=[blk(HEADS * RET_DV), state_spec],
        out_shape=[jax.ShapeDtypeStruct((nb * nc * rows, HEADS * RET_DV), BF16),
                   jax.ShapeDtypeStruct(s0.shape, F32)],
        scratch_shapes=[pltpu.VMEM((seqs, HEADS, RET_DK, RET_DV), F32)],
        compiler_params=_params("parallel", "arbitrary"), name="retention_t%d" % t,
    )(rq, rk, rv, rg, s0, di, dq, dk, ds, g_ret)


def _outproj_kernel(xp_ref, xs_ref, omp_ref, oms_ref, orp_ref, ors_ref, wo_ref, gffn_ref, wr_ref,
                    x2_ref, hne_ref, route_ref, *, n_prompt_tiles):
    from_prompt = pl.program_id(0) < n_prompt_tiles
    x = jnp.where(from_prompt, xp_ref[...], xs_ref[...])
    o_mla = jnp.where(from_prompt, omp_ref[...], oms_ref[...])
    o_ret = jnp.where(from_prompt, orp_ref[...], ors_ref[...])
    half = HEADS * VDIM
    x2 = x + _mm(o_mla, wo_ref[0:half, :]) + _mm(o_ret, wo_ref[half:, :])
    x2_ref[...] = x2
    hn = _rms(x2, gffn_ref[...])
    d = hn.shape[1]
    hne_ref[:, 0:d] = hn
    logits = _mm(hn.astype(BF16), wr_ref[...])

    def col(kk):
        return logits[:, kk:kk + 1]

    best = col(0)
    grp = jnp.zeros(best.shape, jnp.int32)
    for kk in range(1, N_GROUPS):
        upd = col(kk) > best
        grp = jnp.where(upd, kk, grp)
        best = jnp.where(upd, col(kk), best)
    den = jnp.exp(col(0) - best)
    for kk in range(1, N_GROUPS):
        den = den + jnp.exp(col(kk) - best)
    gate = 1.0 / den

    le = []
    for jj in range(EXPERTS_PER_GROUP):
        sel = col(N_GROUPS + (N_GROUPS - 1) * EXPERTS_PER_GROUP + jj)
        for gg in range(N_GROUPS - 2, -1, -1):
            sel = jnp.where(grp == gg, col(N_GROUPS + gg * EXPERTS_PER_GROUP + jj), sel)
        le.append(sel)
    mx = jnp.maximum(jnp.maximum(le[0], le[1]), jnp.maximum(le[2], le[3]))
    ex = [jnp.exp(v - mx) for v in le]
    chosen = []
    for jj in range(EXPERTS_PER_GROUP):
        rank = jnp.zeros(best.shape, jnp.int32)
        for ii in range(EXPERTS_PER_GROUP):
            if ii == jj:
                continue
            ahead = (ex[ii] > ex[jj]) | ((ex[ii] == ex[jj]) & (ii < jj))
            rank = rank + ahead.astype(jnp.int32)
        chosen.append(rank < 2)
    den2 = jnp.zeros(best.shape, F32)
    for jj in range(EXPERTS_PER_GROUP):
        den2 = den2 + jnp.where(chosen[jj], ex[jj], 0.0)
    lane = lax.broadcasted_iota(jnp.int32, logits.shape, 1)
    route = jnp.where(lane == EXPERTS_PER_GROUP, grp.astype(F32), 0.0)
    for jj in range(EXPERTS_PER_GROUP):
        comb = jnp.where(chosen[jj], ex[jj] / den2 * gate, 0.0)
        route = jnp.where(lane == jj, comb, route)
    hne_ref[:, d:] = route
    route_ref[...] = route


def _outproj(x_p, x_s, om_p, om_s, or_p, or_s, w, *, tm):
    tp, d = x_p.shape
    ts = x_s.shape[0]
    npt = tp // tm
    nst = ts // tm
    p_idx = lambda i: (jnp.minimum(i, npt - 1), 0)
    s_idx = lambda i: (jnp.maximum(i - npt, 0), 0)
    mix = om_p.shape[1]
    return pl.pallas_call(
        functools.partial(_outproj_kernel, n_prompt_tiles=npt),
        grid=(npt + nst,),
        in_specs=[pl.BlockSpec((tm, d), p_idx), pl.BlockSpec((tm, d), s_idx),
                  pl.BlockSpec((tm, mix), p_idx), pl.BlockSpec((tm, mix), s_idx),
                  pl.BlockSpec((tm, mix), p_idx), pl.BlockSpec((tm, mix), s_idx),
                  _const_spec(w["w_o"].shape), _const_spec(w["g_ffn"].shape), _const_spec(w["w_r"].shape)],
        out_specs=[pl.BlockSpec((tm, d), lambda i: (i, 0)), pl.BlockSpec((tm, d + LANES), lambda i: (i, 0)),
                   pl.BlockSpec((tm, LANES), lambda i: (i, 0))],
        out_shape=[jax.ShapeDtypeStruct((tp + ts, d), F32), jax.ShapeDtypeStruct((tp + ts, d + LANES), F32),
                   jax.ShapeDtypeStruct((tp + ts, LANES), F32)],
        compiler_params=_params("parallel"), name="outproj_router",
    )(x_p, x_s, om_p, om_s, or_p, or_s, w["w_o"], w["g_ffn"], w["w_r"])


def _gather_rows(idx_ref, base, n, src_hbm, buf, sem):
    def issue(r, carry):
        pltpu.make_async_copy(src_hbm.at[pl.ds(idx_ref[base + r], 1)], buf.at[pl.ds(r, 1)], sem).start()
        return carry
    lax.fori_loop(0, n, issue, 0)
    pltpu.make_async_copy(src_hbm.at[pl.ds(0, n)], buf, sem).wait()


def _moe_kernel(src_ref, tg_ref, tv_ref, hne_hbm, wg_ref, wu_ref, wd_ref, out_ref, buf, sem, *, tmm, d):
    i = pl.program_id(0)

    @pl.when(tv_ref[i] == 1)
    def _():
        _gather_rows(src_ref, i * tmm, tmm, hne_hbm, buf, sem.at[0])
        hb = buf[:, 0:d].astype(BF16)
        acc = jnp.zeros((tmm, d), F32)
        for jj in range(EXPERTS_PER_GROUP):
            a = _silu(_mm(hb, wg_ref[jj])) * _mm(hb, wu_ref[jj])
            a = a * buf[:, d + jj:d + jj + 1]
            acc = acc + _mm(a.astype(BF16), wd_ref[jj])
        out_ref[...] = acc

    @pl.when(tv_ref[i] == 0)
    def _():
        out_ref[...] = jnp.zeros(out_ref.shape, F32)


def _moe(src, tile_group, tile_valid, hne, wg, wu, wd, *, tmm):
    n_rows = src.shape[0]
    nt = n_rows // tmm
    d = hne.shape[1] - LANES
    f = wg.shape[2]
    wspec = lambda shape: pl.BlockSpec(shape, lambda i, s, tg, tv: (tg[i], 0, 0), pipeline_mode=pl.Buffered(1))
    grid_spec = pltpu.PrefetchScalarGridSpec(
        num_scalar_prefetch=3, grid=(nt,),
        in_specs=[pl.BlockSpec(memory_space=pl.ANY),
                  wspec((EXPERTS_PER_GROUP, d, f)), wspec((EXPERTS_PER_GROUP, d, f)), wspec((EXPERTS_PER_GROUP, f, d))],
        out_specs=pl.BlockSpec((tmm, d), lambda i, s, tg, tv: (i, 0)),
        scratch_shapes=[pltpu.VMEM((tmm, d + LANES), F32), pltpu.SemaphoreType.DMA((1,))],
    )
    return pl.pallas_call(
        functools.partial(_moe_kernel, tmm=tmm, d=d),
        grid_spec=grid_spec, out_shape=jax.ShapeDtypeStruct((n_rows, d), F32),
        compiler_params=_params("arbitrary"), name="moe_grouped",
    )(src, tile_group, tile_valid, hne, wg, wu, wd)


def _final_kernel(pos_ref, x2_ref, moe_hbm, gfin_ref, y_ref, buf, sem, *, tm):
    _gather_rows(pos_ref, pl.program_id(0) * tm, tm, moe_hbm, buf, sem.at[0])
    y_ref[...] = _rms(x2_ref[...] + buf[...], gfin_ref[...])


def _final(pos, x2, moe_sorted, g_final, *, tm, tile0):
    n = pos.shape[0]
    d = x2.shape[1]
    grid_spec = pltpu.PrefetchScalarGridSpec(
        num_scalar_prefetch=1, grid=(n // tm,),
        in_specs=[pl.BlockSpec((tm, d), lambda i, p: (i + tile0, 0)),
                  pl.BlockSpec(memory_space=pl.ANY),
                  pl.BlockSpec(g_final.shape, lambda i, p: (0, 0))],
        out_specs=pl.BlockSpec((tm, d), lambda i, p: (i, 0)),
        scratch_shapes=[pltpu.VMEM((tm, d), F32), pltpu.SemaphoreType.DMA((1,))],
    )
    return pl.pallas_call(
        functools.partial(_final_kernel, tm=tm),
        grid_spec=grid_spec, out_shape=jax.ShapeDtypeStruct((n, d), F32),
        compiler_params=_params("arbitrary"), name="final_norm",
    )(pos, x2, moe_sorted, g_final)


def _rope_tables(pos):
    inv = 1.0 / (ROPE_BASE ** (jnp.arange(0, ROPE, 2, dtype=F32) / ROPE))
    ang = pos.astype(F32)[:, None] * inv[None, :]
    cos = jnp.cos(ang)
    sin = jnp.sin(ang)
    return jnp.concatenate([cos] * 4, axis=1), jnp.concatenate([-sin, sin] * 2, axis=1)


def _sort_by_group(group, tmm):
    n = group.shape[0]
    onehot = (group[:, None] == jnp.arange(N_GROUPS, dtype=jnp.int32)[None, :]).astype(jnp.int32)
    csum = jnp.cumsum(onehot, axis=0)
    counts = csum[-1]
    rank = jnp.take_along_axis(csum, group[:, None], axis=1)[:, 0] - 1
    padded = ((counts + tmm - 1) // tmm) * tmm
    ends = jnp.cumsum(padded)
    pos = (ends - padded)[group] + rank
    n_tiles = n // tmm + N_GROUPS
    src = jnp.zeros((n_tiles * tmm,), jnp.int32).at[pos].set(jnp.arange(n, dtype=jnp.int32))
    starts = jnp.arange(n_tiles, dtype=jnp.int32) * tmm
    tg = jnp.sum((starts[:, None] >= ends[None, :]).astype(jnp.int32), axis=1)
    valid = (tg < N_GROUPS).astype(jnp.int32)
    return pos.astype(jnp.int32), src, jnp.minimum(tg, N_GROUPS - 1).astype(jnp.int32), valid


def kernel(x_prompt, x_sample, cache_latent, cache_kpe, state_ret, page_table, g_attn, w_in, g_q_a, w_q_b,
           g_kv_a, w_uk, w_uv, g_ret, w_o, g_ffn, w_router_group, w_router_expert, w_exp_gate, w_exp_up,
           w_exp_down, g_final):
    batch, seq, d = x_prompt.shape
    db, t_new, _ = x_sample.shape
    depth = w_in.shape[0]
    assert depth == 1, "single-layer step"
    n_pool = cache_latent.shape[1]
    past_len = page_table.shape[1] * PAGE
    tp, ts = batch * seq, db * t_new
    tm = min(TM_PROJ, seq, ts)
    assert seq % tm == 0 and ts % tm == 0 and tm % TQ_ATTN == 0 and (tp + ts) % TM_MOE == 0
    pages = min(PAGES_PER_STEP, page_table.shape[1])
    sub = min(PAGES_PER_CHAIN, pages)
    assert page_table.shape[1] % pages == 0 and pages % sub == 0
    seqs = min(RET_SAMPLE_SEQS, db)
    assert db % seqs == 0 and seq % PAGE == 0

    wi = w_in[0]
    q_a, kv_a, k_pe, rq, rk, rv, rg = (wi[:, 0:512], wi[:, 512:1024], wi[:, 1024:1088], wi[:, 1088:1600],
                                       wi[:, 1600:2112], wi[:, 2112:3136], wi[:, 3136:4160])
    w_in_r = jnp.concatenate([q_a, kv_a, rq, rk, rv, rg, k_pe, jnp.zeros((d, LANES - ROPE), F32)], axis=1)
    wqb = w_q_b[0]
    w = {
        "g_attn": g_attn[0][None, :], "w_in": w_in_r.astype(BF16),
        "g_q_a": g_q_a[0][None, :], "g_kv_a": g_kv_a[0][None, :],
        "w_qb": jnp.concatenate([wqb[:, :, :NOPE].reshape(KVL, HEADS * NOPE),
                                 jnp.pad(wqb[:, :, NOPE:], ((0, 0), (0, 0), (0, LANES - ROPE))).reshape(KVL, HEADS * LANES)],
                                axis=1).astype(BF16),
        "w_ukt": jnp.transpose(w_uk[0], (1, 2, 0)).astype(BF16),
        "w_o": w_o[0].astype(BF16), "g_ffn": g_ffn[0][None, :],
        "w_r": jnp.concatenate([w_router_group[0], w_router_expert[0],
                                jnp.zeros((d, LANES - N_GROUPS - N_GROUPS * EXPERTS_PER_GROUP), F32)],
                               axis=1).astype(BF16),
    }
    wuv = jnp.transpose(w_uv[0], (1, 0, 2)).astype(BF16)
    wg, wu, wd = w_exp_gate[0].astype(BF16), w_exp_up[0].astype(BF16), w_exp_down[0].astype(BF16)

    cs_p, sn_p = _rope_tables(jnp.arange(seq))
    cs_s, sn_s = _rope_tables(past_len + jnp.arange(t_new))
    cs_s, sn_s = jnp.tile(cs_s, (db, 1)), jnp.tile(sn_s, (db, 1))
    tiles_per_seq = seq // tm

    (c_p, kpe_p, q_p, rq_p, rk_p, rv_p, rg_p, kt, cb) = _inproj(
        x_prompt.reshape(tp, d), cs_p, sn_p, lambda i: i % tiles_per_seq, w, tm=tm, prompt=True)
    om_p = _attn_prompt(q_p, kt, cb, wuv, batch=batch, seq=seq, tq=TQ_ATTN, tk=tm)
    or_p, st_p = _retention(rq_p, rk_p, rv_p, rg_p, jnp.zeros((batch, HEADS, RET_DK, RET_DV), F32), g_ret[0],
                            seqs=1, t=PAGE, nc=seq // PAGE)

    (c_s, kpe_s, q_s, rq_s, rk_s, rv_s, rg_s) = _inproj(
        x_sample.reshape(ts, d), cs_s, sn_s, lambda i: i, w, tm=min(TM_PROJ_SAMPLE, ts), prompt=False)
    kpet_pool = jnp.swapaxes(cache_kpe.reshape(n_pool, PAGE, ROPE), 1, 2)
    o_lat_s = _attn_sample(page_table, q_s, c_s, kpe_s, cache_latent.reshape(n_pool, PAGE, KVL), kpet_pool,
                           t_new=t_new, pages=pages, sub=sub)
    om_s = _uv_sample(o_lat_s, wuv, t_new=t_new)
    or_s, st_s = _retention(rq_s, rk_s, rv_s, rg_s, state_ret[0], g_ret[0], seqs=seqs, t=t_new, nc=1)

    x2, hne, route = _outproj(x_prompt.reshape(tp, d), x_sample.reshape(ts, d), om_p, om_s, or_p, or_s, w, tm=tm)
    pos, src, tile_group, tile_valid = _sort_by_group(route[:, EXPERTS_PER_GROUP].astype(jnp.int32), TM_MOE)
    moe_sorted = _moe(src, tile_group, tile_valid, hne, wg, wu, wd, tmm=TM_MOE)
    gfin = g_final[None, :]
    tmf = min(TM_FINAL, ts)
    y_p = _final(pos[:tp], x2, moe_sorted, gfin, tm=tmf, tile0=0)
    y_s = _final(pos[tp:], x2, moe_sorted, gfin, tm=tmf, tile0=tp // tmf)

    return (y_p.reshape(batch, seq, d), y_s.reshape(db, t_new, d),
            c_p.reshape(1, batch, seq, KVL), kpe_p.reshape(1, batch, seq, ROPE), st_p[None],
            c_s.reshape(1, db, t_new, KVL), kpe_s.reshape(1, db, t_new, ROPE), st_s[None])
```

```python
import functools

import jax
import jax.numpy as jnp
from jax import lax
from jax.experimental import pallas as pl
from jax.experimental.pallas import tpu as pltpu

F32 = jnp.float32
BF16 = jnp.bfloat16

HEADS = 8
NOPE = 128
ROPE = 64
KVL = 512
VDIM = 128
RET_DK = 64
RET_DV = 128
PAGE = 128
N_GROUPS = 4
EXPERTS_PER_GROUP = 4
EPS = 1e-6
ROPE_BASE = 10000.0
NEG = -1e30
LOG2E = 1.4426950408889634
Q_SCALE = (NOPE + ROPE) ** -0.5 * LOG2E
QK_W = KVL + 128

V7X_VMEM_BYTES = 64 * 1024 * 1024
VMEM_LIMIT = V7X_VMEM_BYTES - 4 * 1024 * 1024
LANES = 128

TM_PROJ = 512
TM_PROJ_SAMPLE = 256
TQ_ATTN = 128
PAGES_PER_STEP = 32
PAGES_PER_CHAIN = 32
RET_SAMPLE_SEQS = 16
TM_MOE = 256
TM_FINAL = 256


def _mm(a, b):
    return jnp.dot(a, b, preferred_element_type=F32)


def _mm_nt(a, b):
    return lax.dot_general(a, b, (((1,), (1,)), ((), ())), preferred_element_type=F32)


def _mm_tn(a, b):
    return lax.dot_general(a, b, (((0,), (0,)), ((), ())), preferred_element_type=F32)


def _rms(x, g):
    return x * lax.rsqrt(jnp.mean(x * x, axis=-1, keepdims=True) + EPS) * g


def _silu(x):
    return x * (1.0 / (1.0 + jnp.exp(-x)))


def _rope(x, cos, sin):
    w = x.shape[-1]
    lane = lax.broadcasted_iota(jnp.int32, x.shape, 1)
    first_half = (lane % ROPE) < (ROPE // 2)
    swapped = jnp.where(first_half, pltpu.roll(x, w - ROPE // 2, 1), pltpu.roll(x, ROPE // 2, 1))
    return x * cos + swapped * sin


def _const_spec(shape):
    nd = len(shape)
    return pl.BlockSpec(shape, lambda *_: (0,) * nd, pipeline_mode=pl.Buffered(1))


def _params(*sem):
    return pltpu.CompilerParams(dimension_semantics=sem, vmem_limit_bytes=VMEM_LIMIT)


def _inproj_kernel(x_ref, cs_ref, sn_ref, gattn_ref, win_ref, gq_ref, gkv_ref, wqb_ref, wukt_ref,
                   c_ref, kpe_ref, q_ref, rq_ref, rk_ref, rv_ref, rg_ref, *rest, prompt):
    h = _rms(x_ref[...], gattn_ref[...]).astype(BF16)
    cs = cs_ref[...]
    sn = sn_ref[...]
    cs4 = jnp.concatenate([cs] * 4, axis=1)
    sn4 = jnp.concatenate([sn] * 4, axis=1)
    cs8 = jnp.concatenate([cs4] * 2, axis=1)
    sn8 = jnp.concatenate([sn4] * 2, axis=1)

    def proj(lo, hi):
        return _mm(h, win_ref[:, lo:hi])

    c = _rms(proj(512, 1024), gkv_ref[...])
    c_ref[...] = c
    kpe128 = _rope(proj(4096, 4224), cs, sn)
    kpe_ref[...] = kpe128[:, :ROPE]

    qn = _rms(proj(0, 512), gq_ref[...]).astype(BF16)
    q = _mm(qn, wqb_ref[...])
    q_pe = _rope(q[:, HEADS * NOPE:], cs8, sn8) * Q_SCALE
    for hh in range(HEADS):
        q_nope = q[:, hh * NOPE:(hh + 1) * NOPE].astype(BF16)
        q_lat = _mm(q_nope, wukt_ref[hh]) * Q_SCALE
        q_ref[hh] = jnp.concatenate([q_lat, q_pe[:, hh * LANES:(hh + 1) * LANES]], axis=1).astype(q_ref.dtype)

    rq_ref[...] = (_rope(proj(1024, 1536), cs4, sn4) * (RET_DK ** -0.5)).astype(rq_ref.dtype)
    rk_ref[...] = _rope(proj(1536, 2048), cs4, sn4).astype(rk_ref.dtype)
    rv_ref[...] = proj(2048, 3072).astype(rv_ref.dtype)
    rg_ref[...] = _silu(proj(3072, 4096)).astype(rg_ref.dtype)

    if prompt:
        kt_ref, cb_ref = rest
        cb_ref[...] = c.astype(BF16)
        kt_ref[0] = jnp.concatenate([c.T, kpe128.T], axis=0).astype(BF16)


def _inproj(x2d, cs, sn, cs_index, w, *, tm, prompt):
    t_tok, d = x2d.shape
    nt = t_tok // tm
    adt = BF16 if prompt else F32
    row = lambda width: pl.BlockSpec((tm, width), lambda i: (i, 0))
    out_shape = [
        jax.ShapeDtypeStruct((t_tok, KVL), F32), jax.ShapeDtypeStruct((t_tok, ROPE), F32),
        jax.ShapeDtypeStruct((HEADS, t_tok, QK_W), adt),
        jax.ShapeDtypeStruct((t_tok, HEADS * RET_DK), adt), jax.ShapeDtypeStruct((t_tok, HEADS * RET_DK), adt),
        jax.ShapeDtypeStruct((t_tok, HEADS * RET_DV), adt), jax.ShapeDtypeStruct((t_tok, HEADS * RET_DV), adt),
    ]
    out_specs = [
        row(KVL), row(ROPE),
        pl.BlockSpec((HEADS, tm, QK_W), lambda i: (0, i, 0)),
        row(HEADS * RET_DK), row(HEADS * RET_DK), row(HEADS * RET_DV), row(HEADS * RET_DV),
    ]
    if prompt:
        out_shape += [jax.ShapeDtypeStruct((nt, QK_W, tm), BF16), jax.ShapeDtypeStruct((t_tok, KVL), BF16)]
        out_specs += [pl.BlockSpec((1, QK_W, tm), lambda i: (i, 0, 0)), row(KVL)]
    in_specs = [
        row(d),
        pl.BlockSpec((tm, LANES), lambda i: (cs_index(i), 0)), pl.BlockSpec((tm, LANES), lambda i: (cs_index(i), 0)),
        _const_spec(w["g_attn"].shape), _const_spec(w["w_in"].shape), _const_spec(w["g_q_a"].shape),
        _const_spec(w["g_kv_a"].shape), _const_spec(w["w_qb"].shape), _const_spec(w["w_ukt"].shape),
    ]
    return pl.pallas_call(
        functools.partial(_inproj_kernel, prompt=prompt),
        grid=(nt,), in_specs=in_specs, out_specs=out_specs, out_shape=out_shape,
        compiler_params=_params("parallel"), name="inproj_prompt" if prompt else "inproj_sample",
    )(x2d, cs, sn, w["g_attn"], w["w_in"], w["g_q_a"], w["g_kv_a"], w["w_qb"], w["w_ukt"])


def _attn_prompt_kernel(q_ref, kt_ref, cb_ref, wuv_ref, o_ref, m_ref, l_ref, acc_ref, sa_ref, sb_ref, *, tq, tk):
    qi = pl.program_id(1)
    rows = HEADS * tq
    q = q_ref[...].reshape(rows, QK_W)
    m_ref[...] = jnp.full((rows, 1), NEG, F32)
    l_ref[...] = jnp.zeros((rows, 1), F32)
    acc_ref[...] = jnp.zeros((rows, KVL), F32)

    def scores(j, s_ref):
        s_ref[...] = _mm(q, kt_ref[j])

    def consume(s_ref, j, masked):
        s = s_ref[...]
        if masked:
            q_pos = lax.broadcasted_iota(jnp.int32, (rows, tk), 0) % tq + qi * tq
            k_pos = lax.broadcasted_iota(jnp.int32, (rows, tk), 1) + j * tk
            s = jnp.where(k_pos <= q_pos, s, NEG)
        m_prev = m_ref[...]
        m_new = jnp.maximum(m_prev, jnp.max(s, axis=1, keepdims=True))
        alpha = jnp.exp2(m_prev - m_new)
        p = jnp.exp2(s - m_new)
        l_ref[...] = alpha * l_ref[...] + jnp.sum(p, axis=1, keepdims=True)
        v = cb_ref[pl.ds(pl.multiple_of(j * tk, tk), tk), :]
        acc_ref[...] = alpha * acc_ref[...] + _mm(p.astype(BF16), v)
        m_ref[...] = m_new

    last = (qi * tq) // tk
    scores(0, sa_ref)

    def pair(i, carry):
        j = 2 * i
        scores(j + 1, sb_ref)
        consume(sa_ref, j, False)
        scores(j + 2, sa_ref)
        consume(sb_ref, j + 1, False)
        return carry

    lax.fori_loop(0, last // 2, pair, 0)

    @pl.when(last % 2 == 0)
    def _():
        consume(sa_ref, last, True)

    @pl.when(last % 2 == 1)
    def _():
        scores(last, sb_ref)
        consume(sa_ref, last - 1, False)
        consume(sb_ref, last, True)

    o = acc_ref[...] / l_ref[...]
    for hh in range(HEADS):
        oh = o[hh * tq:(hh + 1) * tq].astype(BF16)
        o_ref[:, hh * VDIM:(hh + 1) * VDIM] = _mm(oh, wuv_ref[hh]).astype(o_ref.dtype)


def _attn_prompt(q, kt, cb, wuv, *, batch, seq, tq, tk):
    nq = seq // tq
    nkb = seq // tk
    rows = HEADS * tq
    return pl.pallas_call(
        functools.partial(_attn_prompt_kernel, tq=tq, tk=tk),
        grid=(batch, nq),
        in_specs=[
            pl.BlockSpec((HEADS, tq, QK_W), lambda b, i: (0, b * nq + i, 0)),
            pl.BlockSpec((nkb, QK_W, tk), lambda b, i: (b, 0, 0)),
            pl.BlockSpec((seq, KVL), lambda b, i: (b, 0)),
            _const_spec(wuv.shape),
        ],
        out_specs=pl.BlockSpec((tq, HEADS * VDIM), lambda b, i: (b * nq + i, 0)),
        out_shape=jax.ShapeDtypeStruct((batch * seq, HEADS * VDIM), BF16),
        scratch_shapes=[pltpu.VMEM((rows, 1), F32), pltpu.VMEM((rows, 1), F32), pltpu.VMEM((rows, KVL), F32),
                        pltpu.VMEM((rows, tk), F32), pltpu.VMEM((rows, tk), F32)],
        compiler_params=_params("parallel", "arbitrary"), name="attn_prompt",
    )(q, kt, cb, wuv)


def _attn_sample_kernel(pt_ref, q_ref, cnew_ref, knew_ref, lat_hbm, kpet_hbm, o_ref,
                        cbuf, kbuf, sem, qt_ref, qpe_ref, m_ref, l_ref, acc_ref, *, pages, sub, nblk, t_new):
    b = pl.program_id(0)
    j = pl.program_id(1)
    step = b * nblk + j
    total = pl.num_programs(0) * nblk
    slot = step % 2
    rows = HEADS * t_new

    def start_pages(bb, jj, sl):
        for p in range(pages):
            page = pt_ref[bb, jj * pages + p]
            pltpu.make_async_copy(lat_hbm.at[page], cbuf.at[sl, p], sem.at[0, sl]).start()
            pltpu.make_async_copy(kpet_hbm.at[page], kbuf.at[sl, p], sem.at[1, sl]).start()

    def wait_pages(sl):
        pltpu.make_async_copy(lat_hbm.at[pl.ds(0, pages)], cbuf.at[sl], sem.at[0, sl]).wait()
        pltpu.make_async_copy(kpet_hbm.at[pl.ds(0, pages)], kbuf.at[sl], sem.at[1, sl]).wait()

    @pl.when(step == 0)
    def _():
        start_pages(0, 0, 0)

    @pl.when(step + 1 < total)
    def _():
        nxt = step + 1
        start_pages(nxt // nblk, nxt % nblk, 1 - slot)

    wait_pages(slot)

    @pl.when(j == 0)
    def _():
        q = q_ref[...].reshape(rows, QK_W)
        zpad = jnp.zeros((LANES - rows, QK_W), F32)
        qt_ref[...] = jnp.concatenate([q, zpad], axis=0).T.astype(BF16)
        qpe_ref[...] = q[:, KVL:KVL + ROPE].astype(BF16)
        m_ref[...] = jnp.full((rows, 1), NEG, F32)
        l_ref[...] = jnp.zeros((rows, 1), F32)
        acc_ref[...] = jnp.zeros((rows, KVL), F32)

    def partial_softmax(s, v):
        m_c = jnp.max(s, axis=1, keepdims=True)
        p = jnp.exp2(s - m_c)
        return m_c, jnp.sum(p, axis=1, keepdims=True), _mm(p.astype(BF16), v)

    def merge(parts):
        m_prev = m_ref[...]
        m_new = m_prev
        for m_c, _, _ in parts:
            m_new = jnp.maximum(m_new, m_c)
        alpha = jnp.exp2(m_prev - m_new)
        l_new = alpha * l_ref[...]
        acc = alpha * acc_ref[...]
        for m_c, l_c, a_c in parts:
            w_c = jnp.exp2(m_c - m_new)
            l_new = l_new + w_c * l_c
            acc = acc + w_c * a_c
        m_ref[...] = m_new
        l_ref[...] = l_new
        acc_ref[...] = acc

    q_lat_t = qt_ref[0:KVL, :]
    q_pe = qpe_ref[...]
    parts = []
    for c0 in range(0, pages, sub):
        kc = cbuf[slot, c0:c0 + sub].reshape(sub * PAGE, KVL).astype(BF16)
        kpt = jnp.concatenate([kbuf[slot, c0 + p] for p in range(sub)], axis=1).astype(BF16)
        parts.append(partial_softmax(_mm(kc, q_lat_t).T[:rows] + _mm(q_pe, kpt), kc))
    merge(parts)

    @pl.when(j == nblk - 1)
    def _():
        zc = jnp.zeros((LANES - t_new, KVL), F32)
        zk = jnp.zeros((LANES - t_new, ROPE), F32)
        cn = jnp.concatenate([cnew_ref[...], zc], axis=0).astype(BF16)
        kn = jnp.concatenate([knew_ref[...], zk], axis=0).astype(BF16)
        s2 = _mm(cn, q_lat_t).T[:rows] + _mm_nt(q_pe, kn)
        q_t = lax.broadcasted_iota(jnp.int32, (rows, LANES), 0) % t_new
        k_t = lax.broadcasted_iota(jnp.int32, (rows, LANES), 1)
        merge([partial_softmax(jnp.where(k_t <= q_t, s2, NEG), cn)])
        o_ref[0] = acc_ref[...] / l_ref[...]


def _attn_sample(page_table, q, c_new, kpe_new, lat_pool, kpet_pool, *, t_new, pages, sub):
    db, n_pages = page_table.shape
    nblk = n_pages // pages
    rows = HEADS * t_new
    grid_spec = pltpu.PrefetchScalarGridSpec(
        num_scalar_prefetch=1, grid=(db, nblk),
        in_specs=[
            pl.BlockSpec((HEADS, t_new, QK_W), lambda b, j, pt: (0, b, 0)),
            pl.BlockSpec((t_new, KVL), lambda b, j, pt: (b, 0)),
            pl.BlockSpec((t_new, ROPE), lambda b, j, pt: (b, 0)),
            pl.BlockSpec(memory_space=pl.ANY), pl.BlockSpec(memory_space=pl.ANY),
        ],
        out_specs=pl.BlockSpec((1, rows, KVL), lambda b, j, pt: (b, 0, 0)),
        scratch_shapes=[
            pltpu.VMEM((2, pages, PAGE, KVL), F32), pltpu.VMEM((2, pages, ROPE, PAGE), F32),
            pltpu.SemaphoreType.DMA((2, 2)),
            pltpu.VMEM((QK_W, LANES), BF16), pltpu.VMEM((rows, ROPE), BF16),
            pltpu.VMEM((rows, 1), F32), pltpu.VMEM((rows, 1), F32), pltpu.VMEM((rows, KVL), F32),
        ],
    )
    return pl.pallas_call(
        functools.partial(_attn_sample_kernel, pages=pages, sub=sub, nblk=nblk, t_new=t_new),
        grid_spec=grid_spec, out_shape=jax.ShapeDtypeStruct((db, rows, KVL), F32),
        compiler_params=_params("arbitrary", "arbitrary"), name="attn_sample",
    )(page_table, q, c_new, kpe_new, lat_pool, kpet_pool)


def _uv_sample_kernel(o_ref, wuv_ref, out_ref):
    n, t_new, _ = o_ref.shape
    out_ref[...] = _mm(o_ref[...].reshape(n * t_new, KVL).astype(BF16), wuv_ref[0]).astype(out_ref.dtype)


def _uv_sample(o_lat, wuv, *, t_new):
    db = o_lat.shape[0]
    o4 = o_lat.reshape(db, HEADS, t_new, KVL)
    return pl.pallas_call(
        _uv_sample_kernel, grid=(HEADS,),
        in_specs=[pl.BlockSpec((db, None, t_new, KVL), lambda h: (0, h, 0, 0)),
                  pl.BlockSpec((1, KVL, VDIM), lambda h: (h, 0, 0))],
        out_specs=pl.BlockSpec((db * t_new, VDIM), lambda h: (0, h)),
        out_shape=jax.ShapeDtypeStruct((db * t_new, HEADS * VDIM), BF16),
        compiler_params=_params("parallel"), name="uv_sample",
    )(o4, wuv)


def _ret_kernel(rq_ref, rk_ref, rv_ref, rg_ref, s0_ref, di_ref, dq_ref, dk_ref, ds_ref, gr_ref,
                o_ref, sf_ref, st_ref, *, seqs, t, nc):
    c = pl.program_id(1)

    @pl.when(c == 0)
    def _():
        st_ref[...] = s0_ref[...]

    q = rq_ref[...]
    k = rk_ref[...]
    v = rv_ref[...]
    outs = []
    for hh in range(HEADS):
        qh = q[:, hh * RET_DK:(hh + 1) * RET_DK]
        kh = k[:, hh * RET_DK:(hh + 1) * RET_DK]
        vh = v[:, hh * RET_DV:(hh + 1) * RET_DV]
        a = _mm_nt(qh.astype(BF16), kh.astype(BF16)) * di_ref[hh]
        o = _mm(a.astype(BF16), vh.astype(BF16))
        kd = kh.astype(F32) * dk_ref[hh]
        cross = []
        for g in range(seqs):
            sl = slice(g * t, (g + 1) * t)
            state = st_ref[g, hh]
            cross.append(_mm(qh[sl].astype(BF16), state.astype(BF16)))
            st_ref[g, hh] = state * ds_ref[hh] + _mm_tn(kd[sl].astype(BF16), vh[sl].astype(BF16))
        o = o + (cross[0] if seqs == 1 else jnp.concatenate(cross, axis=0)) * dq_ref[hh]
        outs.append(_rms(o, gr_ref[hh:hh + 1, :]))
    o_ref[...] = (rg_ref[...].astype(F32) * jnp.concatenate(outs, axis=1)).astype(o_ref.dtype)

    @pl.when(c == nc - 1)
    def _():
        sf_ref[...] = st_ref[...]


def _retention(rq, rk, rv, rg, s0, g_ret, *, seqs, t, nc):
    rows = seqs * t
    n_state = s0.shape[0]
    nb = n_state // seqs
    hh = jnp.arange(HEADS, dtype=F32)
    log_g = jnp.log1p(-jnp.exp2(-5.0 - hh))
    i = jnp.arange(t, dtype=F32)
    diff = i[:, None] - i[None, :]
    d_intra = jnp.where(diff >= 0, jnp.exp(jnp.maximum(diff, 0.0)[None] * log_g[:, None, None]), 0.0)
    d_q = jnp.exp((i[None, :] + 1.0) * log_g[:, None])
    d_k = jnp.exp((t - 1.0 - i)[None, :] * log_g[:, None])
    d_s = jnp.exp(t * log_g)
    eye = jnp.eye(seqs, dtype=F32)
    di = jnp.einsum("gk,hij->hgikj", eye, d_intra).reshape(HEADS, rows, rows)
    dq = jnp.broadcast_to(jnp.tile(d_q, (1, seqs))[:, :, None], (HEADS, rows, RET_DV))
    dk = jnp.broadcast_to(jnp.tile(d_k, (1, seqs))[:, :, None], (HEADS, rows, RET_DK))
    ds = jnp.broadcast_to(d_s[:, None, None], (HEADS, RET_DK, RET_DV))
    blk = lambda width: pl.BlockSpec((rows, width), lambda s, c: (s * nc + c, 0))
    state_spec = pl.BlockSpec((seqs, HEADS, RET_DK, RET_DV), lambda s, c: (s, 0, 0, 0))
    return pl.pallas_call(
        functools.partial(_ret_kernel, seqs=seqs, t=t, nc=nc),
        grid=(nb, nc),
        in_specs=[blk(HEADS * RET_DK), blk(HEADS * RET_DK), blk(HEADS * RET_DV), blk(HEADS * RET_DV), state_spec,
                  _const_spec(di.shape), _const_spec(dq.shape), _const_spec(dk.shape), _const_spec(ds.shape),
                  _const_spec(g_ret.shape)],
        out_specs=[blk(HEADS * RET_DV), state_spec],
        out_shape=[jax.ShapeDtypeStruct((nb * nc * rows, HEADS * RET_DV), BF16),
                   jax.ShapeDtypeStruct(s0.shape, F32)],
        scratch_shapes=[pltpu.VMEM((seqs, HEADS, RET_DK, RET_DV), F32)],
        compiler_params=_params("parallel", "arbitrary"), name="retention_t%d" % t,
    )(rq, rk, rv, rg, s0, di, dq, dk, ds, g_ret)


def _outproj_kernel(xp_ref, xs_ref, omp_ref, oms_ref, orp_ref, ors_ref, wo_ref, gffn_ref, wr_ref,
                    x2_ref, hn_ref, route_ref, *, n_prompt_tiles):
    from_prompt = pl.program_id(0) < n_prompt_tiles
    x = jnp.where(from_prompt, xp_ref[...], xs_ref[...])
    o_mla = jnp.where(from_prompt, omp_ref[...], oms_ref[...])
    o_ret = jnp.where(from_prompt, orp_ref[...], ors_ref[...])
    half = HEADS * VDIM
    x2 = x + _mm(o_mla, wo_ref[0:half, :]) + _mm(o_ret, wo_ref[half:, :])
    x2_ref[...] = x2
    hn = _rms(x2, gffn_ref[...])
    hn_ref[...] = hn
    logits = _mm(hn.astype(BF16), wr_ref[...])

    def col(kk):
        return logits[:, kk:kk + 1]

    best = col(0)
    grp = jnp.zeros(best.shape, jnp.int32)
    for kk in range(1, N_GROUPS):
        upd = col(kk) > best
        grp = jnp.where(upd, kk, grp)
        best = jnp.where(upd, col(kk), best)
    den = jnp.exp(col(0) - best)
    for kk in range(1, N_GROUPS):
        den = den + jnp.exp(col(kk) - best)
    gate = 1.0 / den

    le = []
    for jj in range(EXPERTS_PER_GROUP):
        sel = col(N_GROUPS + (N_GROUPS - 1) * EXPERTS_PER_GROUP + jj)
        for gg in range(N_GROUPS - 2, -1, -1):
            sel = jnp.where(grp == gg, col(N_GROUPS + gg * EXPERTS_PER_GROUP + jj), sel)
        le.append(sel)
    mx = jnp.maximum(jnp.maximum(le[0], le[1]), jnp.maximum(le[2], le[3]))
    ex = [jnp.exp(v - mx) for v in le]
    chosen = []
    for jj in range(EXPERTS_PER_GROUP):
        rank = jnp.zeros(best.shape, jnp.int32)
        for ii in range(EXPERTS_PER_GROUP):
            if ii == jj:
                continue
            ahead = (ex[ii] > ex[jj]) | ((ex[ii] == ex[jj]) & (ii < jj))
            rank = rank + ahead.astype(jnp.int32)
        chosen.append(rank < 2)
    den2 = jnp.zeros(best.shape, F32)
    for jj in range(EXPERTS_PER_GROUP):
        den2 = den2 + jnp.where(chosen[jj], ex[jj], 0.0)
    lo = jnp.full(best.shape, EXPERTS_PER_GROUP, jnp.int32)
    hi = jnp.full(best.shape, -1, jnp.int32)
    for jj in range(EXPERTS_PER_GROUP):
        lo = jnp.where(chosen[jj], jnp.minimum(lo, jj), lo)
        hi = jnp.where(chosen[jj], jnp.maximum(hi, jj), hi)
    w_lo = jnp.zeros(best.shape, F32)
    w_hi = jnp.zeros(best.shape, F32)
    for jj in range(EXPERTS_PER_GROUP):
        comb = ex[jj] / den2 * gate
        w_lo = jnp.where(lo == jj, comb, w_lo)
        w_hi = jnp.where(hi == jj, comb, w_hi)
    lane = lax.broadcasted_iota(jnp.int32, logits.shape, 1)
    base = grp * EXPERTS_PER_GROUP
    route = jnp.where(lane == 0, w_lo, 0.0)
    route = jnp.where(lane == 1, w_hi, route)
    route = jnp.where(lane == 2, (base + lo).astype(F32), route)
    route = jnp.where(lane == 3, (base + hi).astype(F32), route)
    route_ref[...] = route


def _outproj(x_p, x_s, om_p, om_s, or_p, or_s, w, *, tm):
    tp, d = x_p.shape
    ts = x_s.shape[0]
    npt = tp // tm
    nst = ts // tm
    p_idx = lambda i: (jnp.minimum(i, npt - 1), 0)
    s_idx = lambda i: (jnp.maximum(i - npt, 0), 0)
    mix = om_p.shape[1]
    return pl.pallas_call(
        functools.partial(_outproj_kernel, n_prompt_tiles=npt),
        grid=(npt + nst,),
        in_specs=[pl.BlockSpec((tm, d), p_idx), pl.BlockSpec((tm, d), s_idx),
                  pl.BlockSpec((tm, mix), p_idx), pl.BlockSpec((tm, mix), s_idx),
                  pl.BlockSpec((tm, mix), p_idx), pl.BlockSpec((tm, mix), s_idx),
                  _const_spec(w["w_o"].shape), _const_spec(w["g_ffn"].shape), _const_spec(w["w_r"].shape)],
        out_specs=[pl.BlockSpec((tm, d), lambda i: (i, 0)), pl.BlockSpec((tm, d), lambda i: (i, 0)),
                   pl.BlockSpec((tm, LANES), lambda i: (i, 0))],
        out_shape=[jax.ShapeDtypeStruct((tp + ts, d), F32), jax.ShapeDtypeStruct((tp + ts, d), F32),
                   jax.ShapeDtypeStruct((tp + ts, LANES), F32)],
        compiler_params=_params("parallel"), name="outproj_router",
    )(x_p, x_s, om_p, om_s, or_p, or_s, w["w_o"], w["g_ffn"], w["w_r"])


def _start_row_gather(idx_ref, base, n, src_hbm, buf, sem):
    for r in range(n):
        pltpu.make_async_copy(src_hbm.at[pl.ds(idx_ref[base + r], 1)], buf.at[pl.ds(r, 1)], sem).start()


def _wait_row_gather(n, src_hbm, buf, sem):
    pltpu.make_async_copy(src_hbm.at[pl.ds(0, n)], buf, sem).wait()


def _moe_kernel(src_ref, te_ref, tv_ref, hn_hbm, wg_ref, wu_ref, wd_ref, out_ref, buf, sem, *, tmm):
    i = pl.program_id(0)
    nt = pl.num_programs(0)
    slot = i % 2
    nxt = jnp.minimum(i + 1, nt - 1)

    @pl.when(i == 0)
    def _():
        _start_row_gather(src_ref, 0, tmm, hn_hbm, buf.at[0], sem.at[0])

    @pl.when(tv_ref[i] == 1)
    def _():
        _start_row_gather(src_ref, nxt * tmm, tmm, hn_hbm, buf.at[1 - slot], sem.at[1 - slot])
        _wait_row_gather(tmm, hn_hbm, buf.at[slot], sem.at[slot])
        hb = buf[slot].astype(BF16)
        a = _silu(_mm(hb, wg_ref[0].astype(BF16))) * _mm(hb, wu_ref[0].astype(BF16))
        out_ref[...] = _mm(a.astype(BF16), wd_ref[0].astype(BF16))

    @pl.when(tv_ref[i] == 0)
    def _():
        _start_row_gather(src_ref, nxt * tmm, tmm, hn_hbm, buf.at[1 - slot], sem.at[1 - slot])
        _wait_row_gather(tmm, hn_hbm, buf.at[slot], sem.at[slot])
        out_ref[...] = jnp.zeros(out_ref.shape, F32)

    @pl.when(i == nt - 1)
    def _():
        _wait_row_gather(tmm, hn_hbm, buf.at[1 - slot], sem.at[1 - slot])


def _moe(src, tile_expert, tile_valid, hn, wg, wu, wd, *, tmm):
    n_rows = src.shape[0]
    nt = n_rows // tmm
    d = hn.shape[1]
    f = wg.shape[3]
    wspec = lambda shape: pl.BlockSpec(shape, lambda i, s, te, tv: (0, te[i], 0, 0))
    grid_spec = pltpu.PrefetchScalarGridSpec(
        num_scalar_prefetch=3, grid=(nt,),
        in_specs=[pl.BlockSpec(memory_space=pl.ANY),
                  wspec((None, 1, d, f)), wspec((None, 1, d, f)), wspec((None, 1, f, d))],
        out_specs=pl.BlockSpec((tmm, d), lambda i, s, te, tv: (i, 0)),
        scratch_shapes=[pltpu.VMEM((2, tmm, d), F32), pltpu.SemaphoreType.DMA((2,))],
    )
    return pl.pallas_call(
        functools.partial(_moe_kernel, tmm=tmm),
        grid_spec=grid_spec, out_shape=jax.ShapeDtypeStruct((n_rows, d), F32),
        compiler_params=_params("arbitrary"), name="moe_expert_sorted",
    )(src, tile_expert, tile_valid, hn, wg, wu, wd)


def _final_kernel(pos_ref, x2_ref, route_ref, moe_hbm, gfin_ref, y_ref, buf, sem, *, tm, n_tok):
    i = pl.program_id(0)
    nt = pl.num_programs(0)
    slot = i % 2
    nxt = jnp.minimum(i + 1, nt - 1)

    def start(tile, sl):
        _start_row_gather(pos_ref, tile * tm, tm, moe_hbm, buf.at[sl, 0], sem.at[sl])
        _start_row_gather(pos_ref, n_tok + tile * tm, tm, moe_hbm, buf.at[sl, 1], sem.at[sl])

    def wait(sl):
        _wait_row_gather(tm, moe_hbm, buf.at[sl, 0], sem.at[sl])
        _wait_row_gather(tm, moe_hbm, buf.at[sl, 1], sem.at[sl])

    @pl.when(i == 0)
    def _():
        start(0, 0)

    start(nxt, 1 - slot)
    wait(slot)
    route = route_ref[...]
    moe = route[:, 0:1] * buf[slot, 0] + route[:, 1:2] * buf[slot, 1]
    y_ref[...] = _rms(x2_ref[...] + moe, gfin_ref[...])

    @pl.when(i == nt - 1)
    def _():
        wait(1 - slot)


def _final(pos, x2, route, moe_sorted, g_final, *, tm, tile0):
    n = pos.shape[0] // 2
    d = x2.shape[1]
    grid_spec = pltpu.PrefetchScalarGridSpec(
        num_scalar_prefetch=1, grid=(n // tm,),
        in_specs=[pl.BlockSpec((tm, d), lambda i, p: (i + tile0, 0)),
                  pl.BlockSpec((tm, LANES), lambda i, p: (i + tile0, 0)),
                  pl.BlockSpec(memory_space=pl.ANY),
                  pl.BlockSpec(g_final.shape, lambda i, p: (0, 0))],
        out_specs=pl.BlockSpec((tm, d), lambda i, p: (i, 0)),
        scratch_shapes=[pltpu.VMEM((2, 2, tm, d), F32), pltpu.SemaphoreType.DMA((2,))],
    )
    return pl.pallas_call(
        functools.partial(_final_kernel, tm=tm, n_tok=n),
        grid_spec=grid_spec, out_shape=jax.ShapeDtypeStruct((n, d), F32),
        compiler_params=_params("arbitrary"), name="final_norm",
    )(pos, x2, route, moe_sorted, g_final)


def _rope_tables(pos):
    inv = 1.0 / (ROPE_BASE ** (jnp.arange(0, ROPE, 2, dtype=F32) / ROPE))
    ang = pos.astype(F32)[:, None] * inv[None, :]
    cos = jnp.cos(ang)
    sin = jnp.sin(ang)
    return jnp.concatenate([cos] * 4, axis=1), jnp.concatenate([-sin, sin] * 2, axis=1)


def _sort_by_expert(expert, n_tok, tmm):
    n_exp = N_GROUPS * EXPERTS_PER_GROUP
    n = expert.shape[0]
    onehot = (expert[:, None] == jnp.arange(n_exp, dtype=jnp.int32)[None, :]).astype(jnp.int32)
    csum = jnp.cumsum(onehot, axis=0)
    counts = csum[-1]
    rank = jnp.take_along_axis(csum, expert[:, None], axis=1)[:, 0] - 1
    padded = ((counts + tmm - 1) // tmm) * tmm
    ends = jnp.cumsum(padded)
    pos = ((ends - padded)[expert] + rank).astype(jnp.int32)
    n_tiles = n // tmm + n_exp
    token = jnp.arange(n, dtype=jnp.int32) % n_tok
    src = jnp.zeros((n_tiles * tmm,), jnp.int32).at[pos].set(token)
    starts = jnp.arange(n_tiles, dtype=jnp.int32) * tmm
    te = jnp.sum((starts[:, None] >= ends[None, :]).astype(jnp.int32), axis=1)
    valid = (te < n_exp).astype(jnp.int32)
    return pos, src, jnp.minimum(te, n_exp - 1).astype(jnp.int32), valid


def kernel(x_prompt, x_sample, cache_latent, cache_kpe, state_ret, page_table, g_attn, w_in, g_q_a, w_q_b,
           g_kv_a, w_uk, w_uv, g_ret, w_o, g_ffn, w_router_group, w_router_expert, w_exp_gate, w_exp_up,
           w_exp_down, g_final):
    batch, seq, d = x_prompt.shape
    db, t_new, _ = x_sample.shape
    depth = w_in.shape[0]
    assert depth == 1, "single-layer step"
    n_pool = cache_latent.shape[1]
    past_len = page_table.shape[1] * PAGE
    tp, ts = batch * seq, db * t_new
    tm = min(TM_PROJ, seq, ts)
    assert seq % tm == 0 and ts % tm == 0 and tm % TQ_ATTN == 0 and (tp + ts) % TM_MOE == 0
    pages = min(PAGES_PER_STEP, page_table.shape[1])
    sub = min(PAGES_PER_CHAIN, pages)
    assert page_table.shape[1] % pages == 0 and pages % sub == 0
    seqs = min(RET_SAMPLE_SEQS, db)
    assert db % seqs == 0 and seq % PAGE == 0

    wi = w_in[0]
    q_a, kv_a, k_pe, rq, rk, rv, rg = (wi[:, 0:512], wi[:, 512:1024], wi[:, 1024:1088], wi[:, 1088:1600],
                                       wi[:, 1600:2112], wi[:, 2112:3136], wi[:, 3136:4160])
    w_in_r = jnp.concatenate([q_a, kv_a, rq, rk, rv, rg, k_pe, jnp.zeros((d, LANES - ROPE), F32)], axis=1)
    wqb = w_q_b[0]
    w = {
        "g_attn": g_attn[0][None, :], "w_in": w_in_r.astype(BF16),
        "g_q_a": g_q_a[0][None, :], "g_kv_a": g_kv_a[0][None, :],
        "w_qb": jnp.concatenate([wqb[:, :, :NOPE].reshape(KVL, HEADS * NOPE),
                                 jnp.pad(wqb[:, :, NOPE:], ((0, 0), (0, 0), (0, LANES - ROPE))).reshape(KVL, HEADS * LANES)],
                                axis=1).astype(BF16),
        "w_ukt": jnp.transpose(w_uk[0], (1, 2, 0)).astype(BF16),
        "w_o": w_o[0].astype(BF16), "g_ffn": g_ffn[0][None, :],
        "w_r": jnp.concatenate([w_router_group[0], w_router_expert[0],
                                jnp.zeros((d, LANES - N_GROUPS - N_GROUPS * EXPERTS_PER_GROUP), F32)],
                               axis=1).astype(BF16),
    }
    wuv = jnp.transpose(w_uv[0], (1, 0, 2)).astype(BF16)

    cs_p, sn_p = _rope_tables(jnp.arange(seq))
    cs_s, sn_s = _rope_tables(past_len + jnp.arange(t_new))
    cs_s, sn_s = jnp.tile(cs_s, (db, 1)), jnp.tile(sn_s, (db, 1))
    tiles_per_seq = seq // tm

    (c_p, kpe_p, q_p, rq_p, rk_p, rv_p, rg_p, kt, cb) = _inproj(
        x_prompt.reshape(tp, d), cs_p, sn_p, lambda i: i % tiles_per_seq, w, tm=tm, prompt=True)
    om_p = _attn_prompt(q_p, kt, cb, wuv, batch=batch, seq=seq, tq=TQ_ATTN, tk=tm)
    or_p, st_p = _retention(rq_p, rk_p, rv_p, rg_p, jnp.zeros((batch, HEADS, RET_DK, RET_DV), F32), g_ret[0],
                            seqs=1, t=PAGE, nc=seq // PAGE)

    (c_s, kpe_s, q_s, rq_s, rk_s, rv_s, rg_s) = _inproj(
        x_sample.reshape(ts, d), cs_s, sn_s, lambda i: i, w, tm=min(TM_PROJ_SAMPLE, ts), prompt=False)
    kpet_pool = jnp.swapaxes(cache_kpe.reshape(n_pool, PAGE, ROPE), 1, 2)
    o_lat_s = _attn_sample(page_table, q_s, c_s, kpe_s, cache_latent.reshape(n_pool, PAGE, KVL), kpet_pool,
                           t_new=t_new, pages=pages, sub=sub)
    om_s = _uv_sample(o_lat_s, wuv, t_new=t_new)
    or_s, st_s = _retention(rq_s, rk_s, rv_s, rg_s, state_ret[0], g_ret[0], seqs=seqs, t=t_new, nc=1)

    x2, hn, route = _outproj(x_prompt.reshape(tp, d), x_sample.reshape(ts, d), om_p, om_s, or_p, or_s, w, tm=tm)
    n_tok = tp + ts
    expert = jnp.concatenate([route[:, 2], route[:, 3]]).astype(jnp.int32)
    pos, src, tile_expert, tile_valid = _sort_by_expert(expert, n_tok, TM_MOE)
    moe_sorted = _moe(src, tile_expert, tile_valid, hn, w_exp_gate, w_exp_up, w_exp_down, tmm=TM_MOE)
    gfin = g_final[None, :]
    tmf = min(TM_FINAL, ts)
    pos_p = jnp.concatenate([pos[:tp], pos[n_tok:n_tok + tp]])
    pos_s = jnp.concatenate([pos[tp:n_tok], pos[n_tok + tp:]])
    y_p = _final(pos_p, x2, route, moe_sorted, gfin, tm=tmf, tile0=0)
    y_s = _final(pos_s, x2, route, moe_sorted, gfin, tm=tmf, tile0=tp // tmf)

    return (y_p.reshape(batch, seq, d), y_s.reshape(db, t_new, d),
            c_p.reshape(1, batch, seq, KVL), kpe_p.reshape(1, batch, seq, ROPE), st_p[None],
            c_s.reshape(1, db, t_new, KVL), kpe_s.reshape(1, db, t_new, ROPE), st_s[None])
```

```python
import functools

import jax
import jax.numpy as jnp
from jax import lax
from jax.experimental import pallas as pl
from jax.experimental.pallas import tpu as pltpu

F32 = jnp.float32
BF16 = jnp.bfloat16

HEADS = 8
NOPE = 128
ROPE = 64
KVL = 512
VDIM = 128
RET_DK = 64
RET_DV = 128
PAGE = 128
N_GROUPS = 4
EXPERTS_PER_GROUP = 4
EPS = 1e-6
ROPE_BASE = 10000.0
NEG = -1e30
LOG2E = 1.4426950408889634
Q_SCALE = (NOPE + ROPE) ** -0.5 * LOG2E
QK_W = KVL + 128

V7X_VMEM_BYTES = 64 * 1024 * 1024
VMEM_LIMIT = V7X_VMEM_BYTES - 4 * 1024 * 1024
LANES = 128

TM_PROJ = 512
TM_PROJ_SAMPLE = 256
TQ_ATTN = 128
PAGES_PER_STEP = 32
PAGES_PER_CHAIN = 32
RET_SAMPLE_SEQS = 16
TM_MOE = 256
TM_FINAL = 256


def _mm(a, b):
    return jnp.dot(a, b, preferred_element_type=F32)


def _mm_nt(a, b):
    return lax.dot_general(a, b, (((1,), (1,)), ((), ())), preferred_element_type=F32)


def _mm_tn(a, b):
    return lax.dot_general(a, b, (((0,), (0,)), ((), ())), preferred_element_type=F32)


def _rms(x, g):
    return x * lax.rsqrt(jnp.mean(x * x, axis=-1, keepdims=True) + EPS) * g


def _silu(x):
    return x * (1.0 / (1.0 + jnp.exp(-x)))


def _rope(x, cos, sin):
    w = x.shape[-1]
    lane = lax.broadcasted_iota(jnp.int32, x.shape, 1)
    first_half = (lane % ROPE) < (ROPE // 2)
    swapped = jnp.where(first_half, pltpu.roll(x, w - ROPE // 2, 1), pltpu.roll(x, ROPE // 2, 1))
    return x * cos + swapped * sin


def _const_spec(shape):
    nd = len(shape)
    return pl.BlockSpec(shape, lambda *_: (0,) * nd, pipeline_mode=pl.Buffered(1))


def _params(*sem):
    return pltpu.CompilerParams(dimension_semantics=sem, vmem_limit_bytes=VMEM_LIMIT)


def _inproj_kernel(x_ref, cs_ref, sn_ref, gattn_ref, win_ref, gq_ref, gkv_ref, wqb_ref, wukt_ref,
                   c_ref, kpe_ref, q_ref, rq_ref, rk_ref, rv_ref, rg_ref, *rest, prompt):
    h = _rms(x_ref[...], gattn_ref[...]).astype(BF16)
    cs = cs_ref[...]
    sn = sn_ref[...]
    cs4 = jnp.concatenate([cs] * 4, axis=1)
    sn4 = jnp.concatenate([sn] * 4, axis=1)
    cs8 = jnp.concatenate([cs4] * 2, axis=1)
    sn8 = jnp.concatenate([sn4] * 2, axis=1)

    def proj(lo, hi):
        return _mm(h, win_ref[:, lo:hi])

    c = _rms(proj(512, 1024), gkv_ref[...])
    c_ref[...] = c
    kpe128 = _rope(proj(4096, 4224), cs, sn)
    kpe_ref[...] = kpe128[:, :ROPE]

    qn = _rms(proj(0, 512), gq_ref[...]).astype(BF16)
    q = _mm(qn, wqb_ref[...])
    q_pe = _rope(q[:, HEADS * NOPE:], cs8, sn8) * Q_SCALE
    for hh in range(HEADS):
        q_nope = q[:, hh * NOPE:(hh + 1) * NOPE].astype(BF16)
        q_lat = _mm(q_nope, wukt_ref[hh]) * Q_SCALE
        q_ref[hh] = jnp.concatenate([q_lat, q_pe[:, hh * LANES:(hh + 1) * LANES]], axis=1).astype(q_ref.dtype)

    rq_ref[...] = (_rope(proj(1024, 1536), cs4, sn4) * (RET_DK ** -0.5)).astype(rq_ref.dtype)
    rk_ref[...] = _rope(proj(1536, 2048), cs4, sn4).astype(rk_ref.dtype)
    rv_ref[...] = proj(2048, 3072).astype(rv_ref.dtype)
    rg_ref[...] = _silu(proj(3072, 4096)).astype(rg_ref.dtype)

    if prompt:
        kt_ref, cb_ref = rest
        cb_ref[...] = c.astype(BF16)
        kt_ref[0] = jnp.concatenate([c.T, kpe128.T], axis=0).astype(BF16)


def _inproj(x2d, cs, sn, cs_index, w, *, tm, prompt):
    t_tok, d = x2d.shape
    nt = t_tok // tm
    adt = BF16 if prompt else F32
    row = lambda width: pl.BlockSpec((tm, width), lambda i: (i, 0))
    out_shape = [
        jax.ShapeDtypeStruct((t_tok, KVL), F32), jax.ShapeDtypeStruct((t_tok, ROPE), F32),
        jax.ShapeDtypeStruct((HEADS, t_tok, QK_W), adt),
        jax.ShapeDtypeStruct((t_tok, HEADS * RET_DK), adt), jax.ShapeDtypeStruct((t_tok, HEADS * RET_DK), adt),
        jax.ShapeDtypeStruct((t_tok, HEADS * RET_DV), adt), jax.ShapeDtypeStruct((t_tok, HEADS * RET_DV), adt),
    ]
    out_specs = [
        row(KVL), row(ROPE),
        pl.BlockSpec((HEADS, tm, QK_W), lambda i: (0, i, 0)),
        row(HEADS * RET_DK), row(HEADS * RET_DK), row(HEADS * RET_DV), row(HEADS * RET_DV),
    ]
    if prompt:
        out_shape += [jax.ShapeDtypeStruct((nt, QK_W, tm), BF16), jax.ShapeDtypeStruct((t_tok, KVL), BF16)]
        out_specs += [pl.BlockSpec((1, QK_W, tm), lambda i: (i, 0, 0)), row(KVL)]
    in_specs = [
        row(d),
        pl.BlockSpec((tm, LANES), lambda i: (cs_index(i), 0)), pl.BlockSpec((tm, LANES), lambda i: (cs_index(i), 0)),
        _const_spec(w["g_attn"].shape), _const_spec(w["w_in"].shape), _const_spec(w["g_q_a"].shape),
        _const_spec(w["g_kv_a"].shape), _const_spec(w["w_qb"].shape), _const_spec(w["w_ukt"].shape),
    ]
    return pl.pallas_call(
        functools.partial(_inproj_kernel, prompt=prompt),
        grid=(nt,), in_specs=in_specs, out_specs=out_specs, out_shape=out_shape,
        compiler_params=_params("parallel"), name="inproj_prompt" if prompt else "inproj_sample",
    )(x2d, cs, sn, w["g_attn"], w["w_in"], w["g_q_a"], w["g_kv_a"], w["w_qb"], w["w_ukt"])


def _attn_prompt_kernel(q_ref, kt_ref, cb_ref, wuv_ref, o_ref, m_ref, l_ref, acc_ref, sa_ref, sb_ref, *, tq, tk):
    qi = pl.program_id(1)
    rows = HEADS * tq
    q = q_ref[...].reshape(rows, QK_W)
    m_ref[...] = jnp.full((rows, 1), NEG, F32)
    l_ref[...] = jnp.zeros((rows, 1), F32)
    acc_ref[...] = jnp.zeros((rows, KVL), F32)

    def scores(j, s_ref):
        s_ref[...] = _mm(q, kt_ref[j])

    def consume(s_ref, j, masked):
        s = s_ref[...]
        if masked:
            q_pos = lax.broadcasted_iota(jnp.int32, (rows, tk), 0) % tq + qi * tq
            k_pos = lax.broadcasted_iota(jnp.int32, (rows, tk), 1) + j * tk
            s = jnp.where(k_pos <= q_pos, s, NEG)
        m_prev = m_ref[...]
        m_new = jnp.maximum(m_prev, jnp.max(s, axis=1, keepdims=True))
        alpha = jnp.exp2(m_prev - m_new)
        p = jnp.exp2(s - m_new)
        l_ref[...] = alpha * l_ref[...] + jnp.sum(p, axis=1, keepdims=True)
        v = cb_ref[pl.ds(pl.multiple_of(j * tk, tk), tk), :]
        acc_ref[...] = alpha * acc_ref[...] + _mm(p.astype(BF16), v)
        m_ref[...] = m_new

    last = (qi * tq) // tk
    scores(0, sa_ref)

    def pair(i, carry):
        j = 2 * i
        scores(j + 1, sb_ref)
        consume(sa_ref, j, False)
        scores(j + 2, sa_ref)
        consume(sb_ref, j + 1, False)
        return carry

    lax.fori_loop(0, last // 2, pair, 0)

    @pl.when(last % 2 == 0)
    def _():
        consume(sa_ref, last, True)

    @pl.when(last % 2 == 1)
    def _():
        scores(last, sb_ref)
        consume(sa_ref, last - 1, False)
        consume(sb_ref, last, True)

    o = acc_ref[...] / l_ref[...]
    for hh in range(HEADS):
        oh = o[hh * tq:(hh + 1) * tq].astype(BF16)
        o_ref[:, hh * VDIM:(hh + 1) * VDIM] = _mm(oh, wuv_ref[hh]).astype(o_ref.dtype)


def _attn_prompt(q, kt, cb, wuv, *, batch, seq, tq, tk):
    nq = seq // tq
    nkb = seq // tk
    rows = HEADS * tq
    return pl.pallas_call(
        functools.partial(_attn_prompt_kernel, tq=tq, tk=tk),
        grid=(batch, nq),
        in_specs=[
            pl.BlockSpec((HEADS, tq, QK_W), lambda b, i: (0, b * nq + i, 0)),
            pl.BlockSpec((nkb, QK_W, tk), lambda b, i: (b, 0, 0)),
            pl.BlockSpec((seq, KVL), lambda b, i: (b, 0)),
            _const_spec(wuv.shape),
        ],
        out_specs=pl.BlockSpec((tq, HEADS * VDIM), lambda b, i: (b * nq + i, 0)),
        out_shape=jax.ShapeDtypeStruct((batch * seq, HEADS * VDIM), BF16),
        scratch_shapes=[pltpu.VMEM((rows, 1), F32), pltpu.VMEM((rows, 1), F32), pltpu.VMEM((rows, KVL), F32),
                        pltpu.VMEM((rows, tk), F32), pltpu.VMEM((rows, tk), F32)],
        compiler_params=_params("parallel", "arbitrary"), name="attn_prompt",
    )(q, kt, cb, wuv)


def _attn_sample_kernel(pt_ref, q_ref, cnew_ref, knew_ref, lat_hbm, kpet_hbm, o_ref,
                        cbuf, kbuf, sem, qt_ref, qpe_ref, m_ref, l_ref, acc_ref, *, pages, sub, nblk, t_new):
    b = pl.program_id(0)
    j = pl.program_id(1)
    step = b * nblk + j
    total = pl.num_programs(0) * nblk
    slot = step % 2
    rows = HEADS * t_new

    def start_pages(bb, jj, sl):
        for p in range(pages):
            page = pt_ref[bb, jj * pages + p]
            pltpu.make_async_copy(lat_hbm.at[page], cbuf.at[sl, p], sem.at[0, sl]).start()
            pltpu.make_async_copy(kpet_hbm.at[page], kbuf.at[sl, p], sem.at[1, sl]).start()

    def wait_pages(sl):
        pltpu.make_async_copy(lat_hbm.at[pl.ds(0, pages)], cbuf.at[sl], sem.at[0, sl]).wait()
        pltpu.make_async_copy(kpet_hbm.at[pl.ds(0, pages)], kbuf.at[sl], sem.at[1, sl]).wait()

    @pl.when(step == 0)
    def _():
        start_pages(0, 0, 0)

    wait_pages(slot)

    @pl.when(j == 0)
    def _():
        q = q_ref[...].reshape(rows, QK_W)
        zpad = jnp.zeros((LANES - rows, QK_W), F32)
        qt_ref[...] = jnp.concatenate([q, zpad], axis=0).T.astype(BF16)
        qpe_ref[...] = q[:, KVL:KVL + ROPE].astype(BF16)
        m_ref[...] = jnp.full((rows, 1), NEG, F32)
        l_ref[...] = jnp.zeros((rows, 1), F32)
        acc_ref[...] = jnp.zeros((rows, KVL), F32)

    def partial_softmax(s, v):
        m_c = jnp.max(s, axis=1, keepdims=True)
        p = jnp.exp2(s - m_c)
        return m_c, jnp.sum(p, axis=1, keepdims=True), _mm(p.astype(BF16), v)

    def merge(parts):
        m_prev = m_ref[...]
        m_new = m_prev
        for m_c, _, _ in parts:
            m_new = jnp.maximum(m_new, m_c)
        alpha = jnp.exp2(m_prev - m_new)
        l_new = alpha * l_ref[...]
        acc = alpha * acc_ref[...]
        for m_c, l_c, a_c in parts:
            w_c = jnp.exp2(m_c - m_new)
            l_new = l_new + w_c * l_c
            acc = acc + w_c * a_c
        m_ref[...] = m_new
        l_ref[...] = l_new
        acc_ref[...] = acc

    nxt = jnp.minimum(step + 1, total - 1)
    start_pages(nxt // nblk, nxt % nblk, 1 - slot)

    q_lat_t = qt_ref[0:KVL, :]
    q_pe = qpe_ref[...]
    parts = []
    for c0 in range(0, pages, sub):
        kc = cbuf[slot, c0:c0 + sub].reshape(sub * PAGE, KVL).astype(BF16)
        kpt = jnp.concatenate([kbuf[slot, c0 + p] for p in range(sub)], axis=1).astype(BF16)
        parts.append(partial_softmax(_mm(kc, q_lat_t).T[:rows] + _mm(q_pe, kpt), kc))
    merge(parts)

    @pl.when(j == nblk - 1)
    def _():
        zc = jnp.zeros((LANES - t_new, KVL), F32)
        zk = jnp.zeros((LANES - t_new, ROPE), F32)
        cn = jnp.concatenate([cnew_ref[...], zc], axis=0).astype(BF16)
        kn = jnp.concatenate([knew_ref[...], zk], axis=0).astype(BF16)
        s2 = _mm(cn, q_lat_t).T[:rows] + _mm_nt(q_pe, kn)
        q_t = lax.broadcasted_iota(jnp.int32, (rows, LANES), 0) % t_new
        k_t = lax.broadcasted_iota(jnp.int32, (rows, LANES), 1)
        merge([partial_softmax(jnp.where(k_t <= q_t, s2, NEG), cn)])
        o_ref[0] = acc_ref[...] / l_ref[...]

    @pl.when(step == total - 1)
    def _():
        wait_pages(1 - slot)


def _attn_sample(page_table, q, c_new, kpe_new, lat_pool, kpet_pool, *, t_new, pages, sub):
    db, n_pages = page_table.shape
    nblk = n_pages // pages
    rows = HEADS * t_new
    grid_spec = pltpu.PrefetchScalarGridSpec(
        num_scalar_prefetch=1, grid=(db, nblk),
        in_specs=[
            pl.BlockSpec((HEADS, t_new, QK_W), lambda b, j, pt: (0, b, 0)),
            pl.BlockSpec((t_new, KVL), lambda b, j, pt: (b, 0)),
            pl.BlockSpec((t_new, ROPE), lambda b, j, pt: (b, 0)),
            pl.BlockSpec(memory_space=pl.ANY), pl.BlockSpec(memory_space=pl.ANY),
        ],
        out_specs=pl.BlockSpec((1, rows, KVL), lambda b, j, pt: (b, 0, 0)),
        scratch_shapes=[
            pltpu.VMEM((2, pages, PAGE, KVL), F32), pltpu.VMEM((2, pages, ROPE, PAGE), F32),
            pltpu.SemaphoreType.DMA((2, 2)),
            pltpu.VMEM((QK_W, LANES), BF16), pltpu.VMEM((rows, ROPE), BF16),
            pltpu.VMEM((rows, 1), F32), pltpu.VMEM((rows, 1), F32), pltpu.VMEM((rows, KVL), F32),
        ],
    )
    return pl.pallas_call(
        functools.partial(_attn_sample_kernel, pages=pages, sub=sub, nblk=nblk, t_new=t_new),
        grid_spec=grid_spec, out_shape=jax.ShapeDtypeStruct((db, rows, KVL), F32),
        compiler_params=_params("arbitrary", "arbitrary"), name="attn_sample",
    )(page_table, q, c_new, kpe_new, lat_pool, kpet_pool)


def _uv_sample_kernel(o_ref, wuv_ref, out_ref):
    n, t_new, _ = o_ref.shape
    out_ref[...] = _mm(o_ref[...].reshape(n * t_new, KVL).astype(BF16), wuv_ref[0]).astype(out_ref.dtype)


def _uv_sample(o_lat, wuv, *, t_new):
    db = o_lat.shape[0]
    o4 = o_lat.reshape(db, HEADS, t_new, KVL)
    return pl.pallas_call(
        _uv_sample_kernel, grid=(HEADS,),
        in_specs=[pl.BlockSpec((db, None, t_new, KVL), lambda h: (0, h, 0, 0)),
                  pl.BlockSpec((1, KVL, VDIM), lambda h: (h, 0, 0))],
        out_specs=pl.BlockSpec((db * t_new, VDIM), lambda h: (0, h)),
        out_shape=jax.ShapeDtypeStruct((db * t_new, HEADS * VDIM), BF16),
        compiler_params=_params("parallel"), name="uv_sample",
    )(o4, wuv)


def _ret_kernel(rq_ref, rk_ref, rv_ref, rg_ref, s0_ref, di_ref, dq_ref, dk_ref, ds_ref, gr_ref,
                o_ref, sf_ref, st_ref, *, seqs, t, nc):
    c = pl.program_id(1)

    @pl.when(c == 0)
    def _():
        st_ref[...] = s0_ref[...]

    q = rq_ref[...]
    k = rk_ref[...]
    v = rv_ref[...]
    outs = []
    for hh in range(HEADS):
        qh = q[:, hh * RET_DK:(hh + 1) * RET_DK]
        kh = k[:, hh * RET_DK:(hh + 1) * RET_DK]
        vh = v[:, hh * RET_DV:(hh + 1) * RET_DV]
        a = _mm_nt(qh.astype(BF16), kh.astype(BF16)) * di_ref[hh]
        o = _mm(a.astype(BF16), vh.astype(BF16))
        kd = kh.astype(F32) * dk_ref[hh]
        cross = []
        for g in range(seqs):
            sl = slice(g * t, (g + 1) * t)
            state = st_ref[g, hh]
            cross.append(_mm(qh[sl].astype(BF16), state.astype(BF16)))
            st_ref[g, hh] = state * ds_ref[hh] + _mm_tn(kd[sl].astype(BF16), vh[sl].astype(BF16))
        o = o + (cross[0] if seqs == 1 else jnp.concatenate(cross, axis=0)) * dq_ref[hh]
        outs.append(_rms(o, gr_ref[hh:hh + 1, :]))
    o_ref[...] = (rg_ref[...].astype(F32) * jnp.concatenate(outs, axis=1)).astype(o_ref.dtype)

    @pl.when(c == nc - 1)
    def _():
        sf_ref[...] = st_ref[...]


def _retention(rq, rk, rv, rg, s0, g_ret, *, seqs, t, nc):
    rows = seqs * t
    n_state = s0.shape[0]
    nb = n_state // seqs
    hh = jnp.arange(HEADS, dtype=F32)
    log_g = jnp.log1p(-jnp.exp2(-5.0 - hh))
    i = jnp.arange(t, dtype=F32)
    diff = i[:, None] - i[None, :]
    d_intra = jnp.where(diff >= 0, jnp.exp(jnp.maximum(diff, 0.0)[None] * log_g[:, None, None]), 0.0)
    d_q = jnp.exp((i[None, :] + 1.0) * log_g[:, None])
    d_k = jnp.exp((t - 1.0 - i)[None, :] * log_g[:, None])
    d_s = jnp.exp(t * log_g)
    eye = jnp.eye(seqs, dtype=F32)
    di = jnp.einsum("gk,hij->hgikj", eye, d_intra).reshape(HEADS, rows, rows)
    dq = jnp.broadcast_to(jnp.tile(d_q, (1, seqs))[:, :, None], (HEADS, rows, RET_DV))
    dk = jnp.broadcast_to(jnp.tile(d_k, (1, seqs))[:, :, None], (HEADS, rows, RET_DK))
    ds = jnp.broadcast_to(d_s[:, None, None], (HEADS, RET_DK, RET_DV))
    blk = lambda width: pl.BlockSpec((rows, width), lambda s, c: (s * nc + c, 0))
    state_spec = pl.BlockSpec((seqs, HEADS, RET_DK, RET_DV), lambda s, c: (s, 0, 0, 0))
    return pl.pallas_call(
        functools.partial(_ret_kernel, seqs=seqs, t=t, nc=nc),
        grid=(nb, nc),
        in_specs=[blk(HEADS * RET_DK), blk(HEADS * RET_DK), blk(HEADS * RET_DV), blk(HEADS * RET_DV), state_spec,
                  _const_spec(di.shape), _const_spec(dq.shape), _const_spec(dk.shape), _const_spec(ds.shape),
                  _const_spec(g_ret.shape)],
        out_specs=[blk(HEADS * RET_DV), state_spec],
        out_shape=[jax.ShapeDtypeStruct((nb * nc * rows, HEADS * RET_DV), BF16),
                   jax.ShapeDtypeStruct(s0.shape, F32)],
        scratch_shapes=[pltpu.VMEM((seqs, HEADS, RET_DK, RET_DV), F32)],
        compiler_params=_params("parallel", "arbitrary"), name="retention_t%d" % t,
    )(rq, rk, rv, rg, s0, di, dq, dk, ds, g_ret)


def _outproj_kernel(xp_ref, xs_ref, omp_ref, oms_ref, orp_ref, ors_ref, wo_ref, gffn_ref, wr_ref,
                    x2_ref, hn_ref, route_ref, *, n_prompt_tiles):
    from_prompt = pl.program_id(0) < n_prompt_tiles
    x = jnp.where(from_prompt, xp_ref[...], xs_ref[...])
    o_mla = jnp.where(from_prompt, omp_ref[...], oms_ref[...])
    o_ret = jnp.where(from_prompt, orp_ref[...], ors_ref[...])
    half = HEADS * VDIM
    x2 = x + _mm(o_mla, wo_ref[0:half, :]) + _mm(o_ret, wo_ref[half:, :])
    x2_ref[...] = x2
    hn = _rms(x2, gffn_ref[...])
    hn_ref[...] = hn
    logits = _mm(hn.astype(BF16), wr_ref[...])

    def col(kk):
        return logits[:, kk:kk + 1]

    best = col(0)
    grp = jnp.zeros(best.shape, jnp.int32)
    for kk in range(1, N_GROUPS):
        upd = col(kk) > best
        grp = jnp.where(upd, kk, grp)
        best = jnp.where(upd, col(kk), best)
    den = jnp.exp(col(0) - best)
    for kk in range(1, N_GROUPS):
        den = den + jnp.exp(col(kk) - best)
    gate = 1.0 / den

    le = []
    for jj in range(EXPERTS_PER_GROUP):
        sel = col(N_GROUPS + (N_GROUPS - 1) * EXPERTS_PER_GROUP + jj)
        for gg in range(N_GROUPS - 2, -1, -1):
            sel = jnp.where(grp == gg, col(N_GROUPS + gg * EXPERTS_PER_GROUP + jj), sel)
        le.append(sel)
    mx = jnp.maximum(jnp.maximum(le[0], le[1]), jnp.maximum(le[2], le[3]))
    ex = [jnp.exp(v - mx) for v in le]
    chosen = []
    for jj in range(EXPERTS_PER_GROUP):
        rank = jnp.zeros(best.shape, jnp.int32)
        for ii in range(EXPERTS_PER_GROUP):
            if ii == jj:
                continue
            ahead = (ex[ii] > ex[jj]) | ((ex[ii] == ex[jj]) & (ii < jj))
            rank = rank + ahead.astype(jnp.int32)
        chosen.append(rank < 2)
    den2 = jnp.zeros(best.shape, F32)
    for jj in range(EXPERTS_PER_GROUP):
        den2 = den2 + jnp.where(chosen[jj], ex[jj], 0.0)
    lo = jnp.full(best.shape, EXPERTS_PER_GROUP, jnp.int32)
    hi = jnp.full(best.shape, -1, jnp.int32)
    for jj in range(EXPERTS_PER_GROUP):
        lo = jnp.where(chosen[jj], jnp.minimum(lo, jj), lo)
        hi = jnp.where(chosen[jj], jnp.maximum(hi, jj), hi)
    w_lo = jnp.zeros(best.shape, F32)
    w_hi = jnp.zeros(best.shape, F32)
    for jj in range(EXPERTS_PER_GROUP):
        comb = ex[jj] / den2 * gate
        w_lo = jnp.where(lo == jj, comb, w_lo)
        w_hi = jnp.where(hi == jj, comb, w_hi)
    lane = lax.broadcasted_iota(jnp.int32, logits.shape, 1)
    base = grp * EXPERTS_PER_GROUP
    route = jnp.where(lane == 0, w_lo, 0.0)
    route = jnp.where(lane == 1, w_hi, route)
    route = jnp.where(lane == 2, (base + lo).astype(F32), route)
    route = jnp.where(lane == 3, (base + hi).astype(F32), route)
    route_ref[...] = route


def _outproj(x_p, x_s, om_p, om_s, or_p, or_s, w, *, tm):
    tp, d = x_p.shape
    ts = x_s.shape[0]
    npt = tp // tm
    nst = ts // tm
    p_idx = lambda i: (jnp.minimum(i, npt - 1), 0)
    s_idx = lambda i: (jnp.maximum(i - npt, 0), 0)
    mix = om_p.shape[1]
    return pl.pallas_call(
        functools.partial(_outproj_kernel, n_prompt_tiles=npt),
        grid=(npt + nst,),
        in_specs=[pl.BlockSpec((tm, d), p_idx), pl.BlockSpec((tm, d), s_idx),
                  pl.BlockSpec((tm, mix), p_idx), pl.BlockSpec((tm, mix), s_idx),
                  pl.BlockSpec((tm, mix), p_idx), pl.BlockSpec((tm, mix), s_idx),
                  _const_spec(w["w_o"].shape), _const_spec(w["g_ffn"].shape), _const_spec(w["w_r"].shape)],
        out_specs=[pl.BlockSpec((tm, d), lambda i: (i, 0)), pl.BlockSpec((tm, d), lambda i: (i, 0)),
                   pl.BlockSpec((tm, LANES), lambda i: (i, 0))],
        out_shape=[jax.ShapeDtypeStruct((tp + ts, d), F32), jax.ShapeDtypeStruct((tp + ts, d), F32),
                   jax.ShapeDtypeStruct((tp + ts, LANES), F32)],
        compiler_params=_params("parallel"), name="outproj_router",
    )(x_p, x_s, om_p, om_s, or_p, or_s, w["w_o"], w["g_ffn"], w["w_r"])


def _start_row_gather(idx_ref, base, n, src_hbm, buf, sem):
    for r in range(n):
        pltpu.make_async_copy(src_hbm.at[pl.ds(idx_ref[base + r], 1)], buf.at[pl.ds(r, 1)], sem).start()


def _wait_row_gather(n, src_hbm, buf, sem):
    pltpu.make_async_copy(src_hbm.at[pl.ds(0, n)], buf, sem).wait()


def _moe_kernel(src_ref, te_ref, tv_ref, tf_ref, hn_hbm, wgt_ref, wut_ref, wd_ref, out_ref, buf, sem, wb_ref, *, tmm):
    i = pl.program_id(0)
    nt = pl.num_programs(0)
    slot = i % 3
    ahead = (i + 2) % 3
    nxt = jnp.minimum(i + 2, nt - 1)

    @pl.when(i == 0)
    def _():
        _start_row_gather(src_ref, 0, tmm, hn_hbm, buf.at[0], sem.at[0])
        _start_row_gather(src_ref, jnp.minimum(1, nt - 1) * tmm, tmm, hn_hbm, buf.at[1], sem.at[1])

    @pl.when(tf_ref[i] == 1)
    def _():
        wb_ref[0] = wgt_ref[0].astype(BF16)
        wb_ref[1] = wut_ref[0].astype(BF16)
        wb_ref[2] = wd_ref[0].astype(BF16)

    @pl.when(tv_ref[i] == 1)
    def _():
        _start_row_gather(src_ref, nxt * tmm, tmm, hn_hbm, buf.at[ahead], sem.at[ahead])
        _wait_row_gather(tmm, hn_hbm, buf.at[slot], sem.at[slot])
        hb = buf[slot].astype(BF16)
        a = _silu(_mm_nt(hb, wb_ref[0])) * _mm_nt(hb, wb_ref[1])
        out_ref[...] = _mm(a.astype(BF16), wb_ref[2])

    @pl.when(tv_ref[i] == 0)
    def _():
        _start_row_gather(src_ref, nxt * tmm, tmm, hn_hbm, buf.at[ahead], sem.at[ahead])
        _wait_row_gather(tmm, hn_hbm, buf.at[slot], sem.at[slot])
        out_ref[...] = jnp.zeros(out_ref.shape, F32)

    @pl.when(i == nt - 1)
    def _():
        _wait_row_gather(tmm, hn_hbm, buf.at[(i + 1) % 3], sem.at[(i + 1) % 3])
        _wait_row_gather(tmm, hn_hbm, buf.at[ahead], sem.at[ahead])


def _moe(src, tile_expert, tile_valid, tile_first, hn, wgt, wut, wd, *, tmm):
    n_rows = src.shape[0]
    nt = n_rows // tmm
    d = hn.shape[1]
    f = wd.shape[2]
    wspec = pl.BlockSpec((None, 1, f, d), lambda i, s, te, tv, tf: (0, te[i], 0, 0))
    grid_spec = pltpu.PrefetchScalarGridSpec(
        num_scalar_prefetch=4, grid=(nt,),
        in_specs=[pl.BlockSpec(memory_space=pl.ANY), wspec, wspec, wspec],
        out_specs=pl.BlockSpec((tmm, d), lambda i, s, te, tv, tf: (i, 0)),
        scratch_shapes=[pltpu.VMEM((3, tmm, d), F32), pltpu.SemaphoreType.DMA((3,)),
                        pltpu.VMEM((3, f, d), BF16)],
    )
    return pl.pallas_call(
        functools.partial(_moe_kernel, tmm=tmm),
        grid_spec=grid_spec, out_shape=jax.ShapeDtypeStruct((n_rows, d), F32),
        compiler_params=_params("arbitrary"), name="moe_expert_sorted",
    )(src, tile_expert, tile_valid, tile_first, hn, wgt, wut, wd)


def _final_kernel(pos_ref, x2_ref, route_ref, moe_hbm, gfin_ref, y_ref, buf, sem, *, tm, n_tok):
    i = pl.program_id(0)
    nt = pl.num_programs(0)
    slot = i % 2
    nxt = jnp.minimum(i + 1, nt - 1)

    def start(tile, sl):
        _start_row_gather(pos_ref, tile * tm, tm, moe_hbm, buf.at[sl, 0], sem.at[sl])
        _start_row_gather(pos_ref, n_tok + tile * tm, tm, moe_hbm, buf.at[sl, 1], sem.at[sl])

    def wait(sl):
        _wait_row_gather(tm, moe_hbm, buf.at[sl, 0], sem.at[sl])
        _wait_row_gather(tm, moe_hbm, buf.at[sl, 1], sem.at[sl])

    @pl.when(i == 0)
    def _():
        start(0, 0)

    start(nxt, 1 - slot)
    wait(slot)
    route = route_ref[...]
    moe = route[:, 0:1] * buf[slot, 0] + route[:, 1:2] * buf[slot, 1]
    y_ref[...] = _rms(x2_ref[...] + moe, gfin_ref[...])

    @pl.when(i == nt - 1)
    def _():
        wait(1 - slot)


def _final(pos, x2, route, moe_sorted, g_final, *, tm, tile0):
    n = pos.shape[0] // 2
    d = x2.shape[1]
    grid_spec = pltpu.PrefetchScalarGridSpec(
        num_scalar_prefetch=1, grid=(n // tm,),
        in_specs=[pl.BlockSpec((tm, d), lambda i, p: (i + tile0, 0)),
                  pl.BlockSpec((tm, LANES), lambda i, p: (i + tile0, 0)),
                  pl.BlockSpec(memory_space=pl.ANY),
                  pl.BlockSpec(g_final.shape, lambda i, p: (0, 0))],
        out_specs=pl.BlockSpec((tm, d), lambda i, p: (i, 0)),
        scratch_shapes=[pltpu.VMEM((2, 2, tm, d), F32), pltpu.SemaphoreType.DMA((2,))],
    )
    return pl.pallas_call(
        functools.partial(_final_kernel, tm=tm, n_tok=n),
        grid_spec=grid_spec, out_shape=jax.ShapeDtypeStruct((n, d), F32),
        compiler_params=_params("arbitrary"), name="final_norm",
    )(pos, x2, route, moe_sorted, g_final)


def _rope_tables(pos):
    inv = 1.0 / (ROPE_BASE ** (jnp.arange(0, ROPE, 2, dtype=F32) / ROPE))
    ang = pos.astype(F32)[:, None] * inv[None, :]
    cos = jnp.cos(ang)
    sin = jnp.sin(ang)
    return jnp.concatenate([cos] * 4, axis=1), jnp.concatenate([-sin, sin] * 2, axis=1)


def _sort_by_expert(expert, n_tok, tmm):
    n_exp = N_GROUPS * EXPERTS_PER_GROUP
    n = expert.shape[0]
    onehot = (expert[:, None] == jnp.arange(n_exp, dtype=jnp.int32)[None, :]).astype(jnp.int32)
    csum = jnp.cumsum(onehot, axis=0)
    counts = csum[-1]
    rank = jnp.take_along_axis(csum, expert[:, None], axis=1)[:, 0] - 1
    padded = ((counts + tmm - 1) // tmm) * tmm
    ends = jnp.cumsum(padded)
    pos = ((ends - padded)[expert] + rank).astype(jnp.int32)
    n_tiles = n // tmm + n_exp
    token = jnp.arange(n, dtype=jnp.int32) % n_tok
    src = jnp.zeros((n_tiles * tmm,), jnp.int32).at[pos].set(token)
    starts = jnp.arange(n_tiles, dtype=jnp.int32) * tmm
    te = jnp.sum((starts[:, None] >= ends[None, :]).astype(jnp.int32), axis=1)
    valid = (te < n_exp).astype(jnp.int32)
    te = jnp.minimum(te, n_exp - 1).astype(jnp.int32)
    first = valid * jnp.concatenate([jnp.ones((1,), jnp.int32), (te[1:] != te[:-1]).astype(jnp.int32)])
    return pos, src, te, valid, first


def kernel(x_prompt, x_sample, cache_latent, cache_kpe, state_ret, page_table, g_attn, w_in, g_q_a, w_q_b,
           g_kv_a, w_uk, w_uv, g_ret, w_o, g_ffn, w_router_group, w_router_expert, w_exp_gate, w_exp_up,
           w_exp_down, g_final):
    batch, seq, d = x_prompt.shape
    db, t_new, _ = x_sample.shape
    depth = w_in.shape[0]
    assert depth == 1, "single-layer step"
    n_pool = cache_latent.shape[1]
    past_len = page_table.shape[1] * PAGE
    tp, ts = batch * seq, db * t_new
    tm = min(TM_PROJ, seq, ts)
    assert seq % tm == 0 and ts % tm == 0 and tm % TQ_ATTN == 0 and (tp + ts) % TM_MOE == 0
    pages = min(PAGES_PER_STEP, page_table.shape[1])
    sub = min(PAGES_PER_CHAIN, pages)
    assert page_table.shape[1] % pages == 0 and pages % sub == 0
    seqs = min(RET_SAMPLE_SEQS, db)
    assert db % seqs == 0 and seq % PAGE == 0

    wi = w_in[0]
    q_a, kv_a, k_pe, rq, rk, rv, rg = (wi[:, 0:512], wi[:, 512:1024], wi[:, 1024:1088], wi[:, 1088:1600],
                                       wi[:, 1600:2112], wi[:, 2112:3136], wi[:, 3136:4160])
    w_in_r = jnp.concatenate([q_a, kv_a, rq, rk, rv, rg, k_pe, jnp.zeros((d, LANES - ROPE), F32)], axis=1)
    wqb = w_q_b[0]
    w = {
        "g_attn": g_attn[0][None, :], "w_in": w_in_r.astype(BF16),
        "g_q_a": g_q_a[0][None, :], "g_kv_a": g_kv_a[0][None, :],
        "w_qb": jnp.concatenate([wqb[:, :, :NOPE].reshape(KVL, HEADS * NOPE),
                                 jnp.pad(wqb[:, :, NOPE:], ((0, 0), (0, 0), (0, LANES - ROPE))).reshape(KVL, HEADS * LANES)],
                                axis=1).astype(BF16),
        "w_ukt": jnp.transpose(w_uk[0], (1, 2, 0)).astype(BF16),
        "w_o": w_o[0].astype(BF16), "g_ffn": g_ffn[0][None, :],
        "w_r": jnp.concatenate([w_router_group[0], w_router_expert[0],
                                jnp.zeros((d, LANES - N_GROUPS - N_GROUPS * EXPERTS_PER_GROUP), F32)],
                               axis=1).astype(BF16),
    }
    wuv = jnp.transpose(w_uv[0], (1, 0, 2)).astype(BF16)

    cs_p, sn_p = _rope_tables(jnp.arange(seq))
    cs_s, sn_s = _rope_tables(past_len + jnp.arange(t_new))
    cs_s, sn_s = jnp.tile(cs_s, (db, 1)), jnp.tile(sn_s, (db, 1))
    tiles_per_seq = seq // tm

    (c_p, kpe_p, q_p, rq_p, rk_p, rv_p, rg_p, kt, cb) = _inproj(
        x_prompt.reshape(tp, d), cs_p, sn_p, lambda i: i % tiles_per_seq, w, tm=tm, prompt=True)
    om_p = _attn_prompt(q_p, kt, cb, wuv, batch=batch, seq=seq, tq=TQ_ATTN, tk=tm)
    or_p, st_p = _retention(rq_p, rk_p, rv_p, rg_p, jnp.zeros((batch, HEADS, RET_DK, RET_DV), F32), g_ret[0],
                            seqs=1, t=PAGE, nc=seq // PAGE)

    (c_s, kpe_s, q_s, rq_s, rk_s, rv_s, rg_s) = _inproj(
        x_sample.reshape(ts, d), cs_s, sn_s, lambda i: i, w, tm=min(TM_PROJ_SAMPLE, ts), prompt=False)
    kpet_pool = jnp.swapaxes(cache_kpe.reshape(n_pool, PAGE, ROPE), 1, 2)
    o_lat_s = _attn_sample(page_table, q_s, c_s, kpe_s, cache_latent.reshape(n_pool, PAGE, KVL), kpet_pool,
                           t_new=t_new, pages=pages, sub=sub)
    om_s = _uv_sample(o_lat_s, wuv, t_new=t_new)
    or_s, st_s = _retention(rq_s, rk_s, rv_s, rg_s, state_ret[0], g_ret[0], seqs=seqs, t=t_new, nc=1)

    x2, hn, route = _outproj(x_prompt.reshape(tp, d), x_sample.reshape(ts, d), om_p, om_s, or_p, or_s, w, tm=tm)
    n_tok = tp + ts
    expert = jnp.concatenate([route[:, 2], route[:, 3]]).astype(jnp.int32)
    pos, src, tile_expert, tile_valid, tile_first = _sort_by_expert(expert, n_tok, TM_MOE)
    moe_sorted = _moe(src, tile_expert, tile_valid, tile_first, hn,
                      jnp.swapaxes(w_exp_gate, 2, 3), jnp.swapaxes(w_exp_up, 2, 3), w_exp_down, tmm=TM_MOE)
    gfin = g_final[None, :]
    tmf = min(TM_FINAL, ts)
    pos_p = jnp.concatenate([pos[:tp], pos[n_tok:n_tok + tp]])
    pos_s = jnp.concatenate([pos[tp:n_tok], pos[n_tok + tp:]])
    y_p = _final(pos_p, x2, route, moe_sorted, gfin, tm=tmf, tile0=0)
    y_s = _final(pos_s, x2, route, moe_sorted, gfin, tm=tmf, tile0=tp // tmf)

    return (y_p.reshape(batch, seq, d), y_s.reshape(db, t_new, d),
            c_p.reshape(1, batch, seq, KVL), kpe_p.reshape(1, batch, seq, ROPE), st_p[None],
            c_s.reshape(1, db, t_new, KVL), kpe_s.reshape(1, db, t_new, ROPE), st_s[None])
```

```python
import functools

import jax
import jax.numpy as jnp
from jax import lax
from jax.experimental import pallas as pl
from jax.experimental.pallas import tpu as pltpu

F32 = jnp.float32
BF16 = jnp.bfloat16

HEADS = 8
NOPE = 128
ROPE = 64
KVL = 512
VDIM = 128
RET_DK = 64
RET_DV = 128
PAGE = 128
N_GROUPS = 4
EXPERTS_PER_GROUP = 4
EPS = 1e-6
ROPE_BASE = 10000.0
NEG = -1e30
LOG2E = 1.4426950408889634
Q_SCALE = (NOPE + ROPE) ** -0.5 * LOG2E
QK_W = KVL + 128

V7X_VMEM_BYTES = 64 * 1024 * 1024
VMEM_LIMIT = V7X_VMEM_BYTES - 4 * 1024 * 1024
LANES = 128

TM_PROJ = 512
TM_PROJ_SAMPLE = 256
TQ_ATTN = 256
PAGES_PER_STEP = 32
PAGES_PER_CHAIN = 32
RET_SAMPLE_SEQS = 16
TM_MOE = 256
TM_FINAL = 256


def _mm(a, b):
    return jnp.dot(a, b, preferred_element_type=F32)


def _mm_nt(a, b):
    return lax.dot_general(a, b, (((1,), (1,)), ((), ())), preferred_element_type=F32)


def _mm_tn(a, b):
    return lax.dot_general(a, b, (((0,), (0,)), ((), ())), preferred_element_type=F32)


def _rms(x, g):
    return x * lax.rsqrt(jnp.mean(x * x, axis=-1, keepdims=True) + EPS) * g


def _silu(x):
    return x * (1.0 / (1.0 + jnp.exp(-x)))


def _rope(x, cos, sin):
    w = x.shape[-1]
    lane = lax.broadcasted_iota(jnp.int32, x.shape, 1)
    first_half = (lane % ROPE) < (ROPE // 2)
    swapped = jnp.where(first_half, pltpu.roll(x, w - ROPE // 2, 1), pltpu.roll(x, ROPE // 2, 1))
    return x * cos + swapped * sin


def _const_spec(shape):
    nd = len(shape)
    return pl.BlockSpec(shape, lambda *_: (0,) * nd, pipeline_mode=pl.Buffered(1))


def _params(*sem):
    return pltpu.CompilerParams(dimension_semantics=sem, vmem_limit_bytes=VMEM_LIMIT)


def _inproj_kernel(x_ref, cs_ref, sn_ref, gattn_ref, win_ref, gq_ref, gkv_ref, wqb_ref, wukt_ref,
                   c_ref, kpe_ref, q_ref, rq_ref, rk_ref, rv_ref, rg_ref, *rest, prompt):
    h = _rms(x_ref[...], gattn_ref[...]).astype(BF16)
    cs = cs_ref[...]
    sn = sn_ref[...]
    cs4 = jnp.concatenate([cs] * 4, axis=1)
    sn4 = jnp.concatenate([sn] * 4, axis=1)
    cs8 = jnp.concatenate([cs4] * 2, axis=1)
    sn8 = jnp.concatenate([sn4] * 2, axis=1)

    def proj(lo, hi):
        return _mm(h, win_ref[:, lo:hi])

    c = _rms(proj(512, 1024), gkv_ref[...])
    c_ref[...] = c
    kpe128 = _rope(proj(4096, 4224), cs, sn)
    kpe_ref[...] = kpe128[:, :ROPE]

    qn = _rms(proj(0, 512), gq_ref[...]).astype(BF16)
    q = _mm(qn, wqb_ref[...])
    q_pe = _rope(q[:, HEADS * NOPE:], cs8, sn8) * Q_SCALE
    for hh in range(HEADS):
        q_nope = q[:, hh * NOPE:(hh + 1) * NOPE].astype(BF16)
        q_lat = _mm(q_nope, wukt_ref[hh]) * Q_SCALE
        q_ref[hh] = jnp.concatenate([q_lat, q_pe[:, hh * LANES:(hh + 1) * LANES]], axis=1).astype(q_ref.dtype)

    rq_ref[...] = (_rope(proj(1024, 1536), cs4, sn4) * (RET_DK ** -0.5)).astype(rq_ref.dtype)
    rk_ref[...] = _rope(proj(1536, 2048), cs4, sn4).astype(rk_ref.dtype)
    rv_ref[...] = proj(2048, 3072).astype(rv_ref.dtype)
    rg_ref[...] = _silu(proj(3072, 4096)).astype(rg_ref.dtype)

    if prompt:
        kt_ref, cb_ref = rest
        cb_ref[...] = c.astype(BF16)
        kt_ref[0] = jnp.concatenate([c.T, kpe128.T], axis=0).astype(BF16)


def _inproj(x2d, cs, sn, cs_index, w, *, tm, prompt):
    t_tok, d = x2d.shape
    nt = t_tok // tm
    adt = BF16 if prompt else F32
    row = lambda width: pl.BlockSpec((tm, width), lambda i: (i, 0))
    out_shape = [
        jax.ShapeDtypeStruct((t_tok, KVL), F32), jax.ShapeDtypeStruct((t_tok, ROPE), F32),
        jax.ShapeDtypeStruct((HEADS, t_tok, QK_W), adt),
        jax.ShapeDtypeStruct((t_tok, HEADS * RET_DK), adt), jax.ShapeDtypeStruct((t_tok, HEADS * RET_DK), adt),
        jax.ShapeDtypeStruct((t_tok, HEADS * RET_DV), adt), jax.ShapeDtypeStruct((t_tok, HEADS * RET_DV), adt),
    ]
    out_specs = [
        row(KVL), row(ROPE),
        pl.BlockSpec((HEADS, tm, QK_W), lambda i: (0, i, 0)),
        row(HEADS * RET_DK), row(HEADS * RET_DK), row(HEADS * RET_DV), row(HEADS * RET_DV),
    ]
    if prompt:
        out_shape += [jax.ShapeDtypeStruct((nt, QK_W, tm), BF16), jax.ShapeDtypeStruct((t_tok, KVL), BF16)]
        out_specs += [pl.BlockSpec((1, QK_W, tm), lambda i: (i, 0, 0)), row(KVL)]
    in_specs = [
        row(d),
        pl.BlockSpec((tm, LANES), lambda i: (cs_index(i), 0)), pl.BlockSpec((tm, LANES), lambda i: (cs_index(i), 0)),
        _const_spec(w["g_attn"].shape), _const_spec(w["w_in"].shape), _const_spec(w["g_q_a"].shape),
        _const_spec(w["g_kv_a"].shape), _const_spec(w["w_qb"].shape), _const_spec(w["w_ukt"].shape),
    ]
    return pl.pallas_call(
        functools.partial(_inproj_kernel, prompt=prompt),
        grid=(nt,), in_specs=in_specs, out_specs=out_specs, out_shape=out_shape,
        compiler_params=_params("parallel"), name="inproj_prompt" if prompt else "inproj_sample",
    )(x2d, cs, sn, w["g_attn"], w["w_in"], w["g_q_a"], w["g_kv_a"], w["w_qb"], w["w_ukt"])


def _attn_prompt_kernel(q_ref, kt_ref, cb_ref, wuv_ref, o_ref, m_ref, l_ref, acc_ref, sa_ref, sb_ref, *, tq, tk):
    qi = pl.program_id(1)
    rows = HEADS * tq
    q = q_ref[...].reshape(rows, QK_W)
    m_ref[...] = jnp.full((rows, 1), NEG, F32)
    l_ref[...] = jnp.zeros((rows, 1), F32)
    acc_ref[...] = jnp.zeros((rows, KVL), F32)

    def scores(j, s_ref):
        s_ref[...] = _mm(q, kt_ref[j])

    def consume(s_ref, j, masked):
        s = s_ref[...]
        if masked:
            q_pos = lax.broadcasted_iota(jnp.int32, (rows, tk), 0) % tq + qi * tq
            k_pos = lax.broadcasted_iota(jnp.int32, (rows, tk), 1) + j * tk
            s = jnp.where(k_pos <= q_pos, s, NEG)
        m_prev = m_ref[...]
        m_new = jnp.maximum(m_prev, jnp.max(s, axis=1, keepdims=True))
        alpha = jnp.exp2(m_prev - m_new)
        p = jnp.exp2(s - m_new)
        l_ref[...] = alpha * l_ref[...] + jnp.sum(p, axis=1, keepdims=True)
        v = cb_ref[pl.ds(pl.multiple_of(j * tk, tk), tk), :]
        acc_ref[...] = alpha * acc_ref[...] + _mm(p.astype(BF16), v)
        m_ref[...] = m_new

    last = (qi * tq) // tk
    scores(0, sa_ref)

    def pair(i, carry):
        j = 2 * i
        scores(j + 1, sb_ref)
        consume(sa_ref, j, False)
        scores(j + 2, sa_ref)
        consume(sb_ref, j + 1, False)
        return carry

    lax.fori_loop(0, last // 2, pair, 0)

    @pl.when(last % 2 == 0)
    def _():
        consume(sa_ref, last, True)

    @pl.when(last % 2 == 1)
    def _():
        scores(last, sb_ref)
        consume(sa_ref, last - 1, False)
        consume(sb_ref, last, True)

    o = acc_ref[...] / l_ref[...]
    for hh in range(HEADS):
        oh = o[hh * tq:(hh + 1) * tq].astype(BF16)
        o_ref[:, hh * VDIM:(hh + 1) * VDIM] = _mm(oh, wuv_ref[hh]).astype(o_ref.dtype)


def _attn_prompt(q, kt, cb, wuv, *, batch, seq, tq, tk):
    nq = seq // tq
    nkb = seq // tk
    rows = HEADS * tq
    return pl.pallas_call(
        functools.partial(_attn_prompt_kernel, tq=tq, tk=tk),
        grid=(batch, nq),
        in_specs=[
            pl.BlockSpec((HEADS, tq, QK_W), lambda b, i: (0, b * nq + i, 0)),
            pl.BlockSpec((nkb, QK_W, tk), lambda b, i: (b, 0, 0)),
            pl.BlockSpec((seq, KVL), lambda b, i: (b, 0)),
            _const_spec(wuv.shape),
        ],
        out_specs=pl.BlockSpec((tq, HEADS * VDIM), lambda b, i: (b * nq + i, 0)),
        out_shape=jax.ShapeDtypeStruct((batch * seq, HEADS * VDIM), BF16),
        scratch_shapes=[pltpu.VMEM((rows, 1), F32), pltpu.VMEM((rows, 1), F32), pltpu.VMEM((rows, KVL), F32),
                        pltpu.VMEM((rows, tk), F32), pltpu.VMEM((rows, tk), F32)],
        compiler_params=_params("parallel", "arbitrary"), name="attn_prompt",
    )(q, kt, cb, wuv)


def _attn_sample_kernel(pt_ref, q_ref, cnew_ref, knew_ref, lat_hbm, kpet_hbm, o_ref,
                        cbuf0, cbuf1, kbuf0, kbuf1, sem, qt_ref, qpe_ref, m_ref, l_ref, acc_ref,
                        *, pages, sub, nblk, t_new):
    cbufs = (cbuf0, cbuf1)
    kbufs = (kbuf0, kbuf1)
    b = pl.program_id(0)
    j = pl.program_id(1)
    step = b * nblk + j
    total = pl.num_programs(0) * nblk
    slot = step % 2
    rows = HEADS * t_new

    def start_pages(bb, jj, sl):
        for p in range(pages):
            page = pt_ref[bb, jj * pages + p]
            pltpu.make_async_copy(lat_hbm.at[page], cbufs[sl].at[p], sem.at[0, sl]).start()
            pltpu.make_async_copy(kpet_hbm.at[page], kbufs[sl].at[p], sem.at[1, sl]).start()

    def wait_pages(sl):
        pltpu.make_async_copy(lat_hbm.at[pl.ds(0, pages)], cbufs[sl], sem.at[0, sl]).wait()
        pltpu.make_async_copy(kpet_hbm.at[pl.ds(0, pages)], kbufs[sl], sem.at[1, sl]).wait()

    @pl.when(step == 0)
    def _():
        start_pages(0, 0, 0)

    @pl.when(j == 0)
    def _():
        q = q_ref[...].reshape(rows, QK_W)
        zpad = jnp.zeros((LANES - rows, QK_W), F32)
        qt_ref[...] = jnp.concatenate([q, zpad], axis=0).T.astype(BF16)
        qpe_ref[...] = q[:, KVL:KVL + ROPE].astype(BF16)
        m_ref[...] = jnp.full((rows, 1), NEG, F32)
        l_ref[...] = jnp.zeros((rows, 1), F32)
        acc_ref[...] = jnp.zeros((rows, KVL), F32)

    def partial_softmax(s, v):
        m_c = jnp.max(s, axis=1, keepdims=True)
        p = jnp.exp2(s - m_c)
        return m_c, jnp.sum(p, axis=1, keepdims=True), _mm(p.astype(BF16), v)

    def merge(parts):
        m_prev = m_ref[...]
        m_new = m_prev
        for m_c, _, _ in parts:
            m_new = jnp.maximum(m_new, m_c)
        alpha = jnp.exp2(m_prev - m_new)
        l_new = alpha * l_ref[...]
        acc = alpha * acc_ref[...]
        for m_c, l_c, a_c in parts:
            w_c = jnp.exp2(m_c - m_new)
            l_new = l_new + w_c * l_c
            acc = acc + w_c * a_c
        m_ref[...] = m_new
        l_ref[...] = l_new
        acc_ref[...] = acc

    nxt = jnp.minimum(step + 1, total - 1)
    for sl in range(2):
        @pl.when(slot == sl)
        def _(sl=sl):
            wait_pages(sl)
            start_pages(nxt // nblk, nxt % nblk, 1 - sl)
            q_lat_t = qt_ref[0:KVL, :]
            q_pe = qpe_ref[...]
            parts = []
            for c0 in range(0, pages, sub):
                kc = cbufs[sl][c0:c0 + sub].reshape(sub * PAGE, KVL).astype(BF16)
                kpt = jnp.concatenate([kbufs[sl][c0 + p] for p in range(sub)], axis=1).astype(BF16)
                parts.append(partial_softmax(_mm(kc, q_lat_t).T[:rows] + _mm(q_pe, kpt), kc))
            merge(parts)

    @pl.when(j == nblk - 1)
    def _():
        q_lat_t = qt_ref[0:KVL, :]
        q_pe = qpe_ref[...]
        zc = jnp.zeros((LANES - t_new, KVL), F32)
        zk = jnp.zeros((LANES - t_new, ROPE), F32)
        cn = jnp.concatenate([cnew_ref[...], zc], axis=0).astype(BF16)
        kn = jnp.concatenate([knew_ref[...], zk], axis=0).astype(BF16)
        s2 = _mm(cn, q_lat_t).T[:rows] + _mm_nt(q_pe, kn)
        q_t = lax.broadcasted_iota(jnp.int32, (rows, LANES), 0) % t_new
        k_t = lax.broadcasted_iota(jnp.int32, (rows, LANES), 1)
        merge([partial_softmax(jnp.where(k_t <= q_t, s2, NEG), cn)])
        o_ref[0] = acc_ref[...] / l_ref[...]

    for sl in range(2):
        @pl.when((step == total - 1) & (slot == sl))
        def _(sl=sl):
            wait_pages(1 - sl)


def _attn_sample(page_table, q, c_new, kpe_new, lat_pool, kpet_pool, *, t_new, pages, sub):
    db, n_pages = page_table.shape
    nblk = n_pages // pages
    rows = HEADS * t_new
    grid_spec = pltpu.PrefetchScalarGridSpec(
        num_scalar_prefetch=1, grid=(db, nblk),
        in_specs=[
            pl.BlockSpec((HEADS, t_new, QK_W), lambda b, j, pt: (0, b, 0)),
            pl.BlockSpec((t_new, KVL), lambda b, j, pt: (b, 0)),
            pl.BlockSpec((t_new, ROPE), lambda b, j, pt: (b, 0)),
            pl.BlockSpec(memory_space=pl.ANY), pl.BlockSpec(memory_space=pl.ANY),
        ],
        out_specs=pl.BlockSpec((1, rows, KVL), lambda b, j, pt: (b, 0, 0)),
        scratch_shapes=[
            pltpu.VMEM((pages, PAGE, KVL), F32), pltpu.VMEM((pages, PAGE, KVL), F32),
            pltpu.VMEM((pages, ROPE, PAGE), F32), pltpu.VMEM((pages, ROPE, PAGE), F32),
            pltpu.SemaphoreType.DMA((2, 2)),
            pltpu.VMEM((QK_W, LANES), BF16), pltpu.VMEM((rows, ROPE), BF16),
            pltpu.VMEM((rows, 1), F32), pltpu.VMEM((rows, 1), F32), pltpu.VMEM((rows, KVL), F32),
        ],
    )
    return pl.pallas_call(
        functools.partial(_attn_sample_kernel, pages=pages, sub=sub, nblk=nblk, t_new=t_new),
        grid_spec=grid_spec, out_shape=jax.ShapeDtypeStruct((db, rows, KVL), F32),
        compiler_params=_params("arbitrary", "arbitrary"), name="attn_sample",
    )(page_table, q, c_new, kpe_new, lat_pool, kpet_pool)


def _uv_sample_kernel(o_ref, wuv_ref, out_ref):
    n, t_new, _ = o_ref.shape
    out_ref[...] = _mm(o_ref[...].reshape(n * t_new, KVL).astype(BF16), wuv_ref[0]).astype(out_ref.dtype)


def _uv_sample(o_lat, wuv, *, t_new):
    db = o_lat.shape[0]
    o4 = o_lat.reshape(db, HEADS, t_new, KVL)
    return pl.pallas_call(
        _uv_sample_kernel, grid=(HEADS,),
        in_specs=[pl.BlockSpec((db, None, t_new, KVL), lambda h: (0, h, 0, 0)),
                  pl.BlockSpec((1, KVL, VDIM), lambda h: (h, 0, 0))],
        out_specs=pl.BlockSpec((db * t_new, VDIM), lambda h: (0, h)),
        out_shape=jax.ShapeDtypeStruct((db * t_new, HEADS * VDIM), BF16),
        compiler_params=_params("parallel"), name="uv_sample",
    )(o4, wuv)


def _ret_kernel(rq_ref, rk_ref, rv_ref, rg_ref, s0_ref, di_ref, dq_ref, dk_ref, ds_ref, gr_ref,
                o_ref, sf_ref, st_ref, *, seqs, t, nc):
    c = pl.program_id(1)

    @pl.when(c == 0)
    def _():
        st_ref[...] = s0_ref[...]

    q = rq_ref[...]
    k = rk_ref[...]
    v = rv_ref[...]
    outs = []
    for hh in range(HEADS):
        qh = q[:, hh * RET_DK:(hh + 1) * RET_DK]
        kh = k[:, hh * RET_DK:(hh + 1) * RET_DK]
        vh = v[:, hh * RET_DV:(hh + 1) * RET_DV]
        a = _mm_nt(qh.astype(BF16), kh.astype(BF16)) * di_ref[hh]
        o = _mm(a.astype(BF16), vh.astype(BF16))
        kd = kh.astype(F32) * dk_ref[hh]
        cross = []
        for g in range(seqs):
            sl = slice(g * t, (g + 1) * t)
            state = st_ref[g, hh]
            cross.append(_mm(qh[sl].astype(BF16), state.astype(BF16)))
            st_ref[g, hh] = state * ds_ref[hh] + _mm_tn(kd[sl].astype(BF16), vh[sl].astype(BF16))
        o = o + (cross[0] if seqs == 1 else jnp.concatenate(cross, axis=0)) * dq_ref[hh]
        outs.append(_rms(o, gr_ref[hh:hh + 1, :]))
    o_ref[...] = (rg_ref[...].astype(F32) * jnp.concatenate(outs, axis=1)).astype(o_ref.dtype)

    @pl.when(c == nc - 1)
    def _():
        sf_ref[...] = st_ref[...]


def _retention(rq, rk, rv, rg, s0, g_ret, *, seqs, t, nc):
    rows = seqs * t
    n_state = s0.shape[0]
    nb = n_state // seqs
    hh = jnp.arange(HEADS, dtype=F32)
    log_g = jnp.log1p(-jnp.exp2(-5.0 - hh))
    i = jnp.arange(t, dtype=F32)
    diff = i[:, None] - i[None, :]
    d_intra = jnp.where(diff >= 0, jnp.exp(jnp.maximum(diff, 0.0)[None] * log_g[:, None, None]), 0.0)
    d_q = jnp.exp((i[None, :] + 1.0) * log_g[:, None])
    d_k = jnp.exp((t - 1.0 - i)[None, :] * log_g[:, None])
    d_s = jnp.exp(t * log_g)
    eye = jnp.eye(seqs, dtype=F32)
    di = jnp.einsum("gk,hij->hgikj", eye, d_intra).reshape(HEADS, rows, rows)
    dq = jnp.broadcast_to(jnp.tile(d_q, (1, seqs))[:, :, None], (HEADS, rows, RET_DV))
    dk = jnp.broadcast_to(jnp.tile(d_k, (1, seqs))[:, :, None], (HEADS, rows, RET_DK))
    ds = jnp.broadcast_to(d_s[:, None, None], (HEADS, RET_DK, RET_DV))
    blk = lambda width: pl.BlockSpec((rows, width), lambda s, c: (s * nc + c, 0))
    state_spec = pl.BlockSpec((seqs, HEADS, RET_DK, RET_DV), lambda s, c: (s, 0, 0, 0))
    return pl.pallas_call(
        functools.partial(_ret_kernel, seqs=seqs, t=t, nc=nc),
        grid=(nb, nc),
        in_specs=[blk(HEADS * RET_DK), blk(HEADS * RET_DK), blk(HEADS * RET_DV), blk(HEADS * RET_DV), state_spec,
                  _const_spec(di.shape), _const_spec(dq.shape), _const_spec(dk.shape), _const_spec(ds.shape),
                  _const_spec(g_ret.shape)],
        out_specs=[blk(HEADS * RET_DV), state_spec],
        out_shape=[jax.ShapeDtypeStruct((nb * nc * rows, HEADS * RET_DV), BF16),
                   jax.ShapeDtypeStruct(s0.shape, F32)],
        scratch_shapes=[pltpu.VMEM((seqs, HEADS, RET_DK, RET_DV), F32)],
        compiler_params=_params("parallel", "arbitrary"), name="retention_t%d" % t,
    )(rq, rk, rv, rg, s0, di, dq, dk, ds, g_ret)


def _outproj_kernel(xp_ref, xs_ref, omp_ref, oms_ref, orp_ref, ors_ref, wo_ref, gffn_ref, wr_ref,
                    x2_ref, hn_ref, route_ref, *, n_prompt_tiles):
    from_prompt = pl.program_id(0) < n_prompt_tiles
    x = jnp.where(from_prompt, xp_ref[...], xs_ref[...])
    o_mla = jnp.where(from_prompt, omp_ref[...], oms_ref[...])
    o_ret = jnp.where(from_prompt, orp_ref[...], ors_ref[...])
    half = HEADS * VDIM
    x2 = x + _mm(o_mla, wo_ref[0:half, :]) + _mm(o_ret, wo_ref[half:, :])
    x2_ref[...] = x2
    hn = _rms(x2, gffn_ref[...])
    hn_ref[...] = hn
    logits = _mm(hn.astype(BF16), wr_ref[...])
    logits_t = logits.T

    def col(kk):
        return logits_t[kk:kk + 1, :]

    best = col(0)
    grp = jnp.zeros(best.shape, jnp.int32)
    for kk in range(1, N_GROUPS):
        upd = col(kk) > best
        grp = jnp.where(upd, kk, grp)
        best = jnp.where(upd, col(kk), best)
    den = jnp.exp(col(0) - best)
    for kk in range(1, N_GROUPS):
        den = den + jnp.exp(col(kk) - best)
    gate = 1.0 / den

    le = []
    for jj in range(EXPERTS_PER_GROUP):
        sel = col(N_GROUPS + (N_GROUPS - 1) * EXPERTS_PER_GROUP + jj)
        for gg in range(N_GROUPS - 2, -1, -1):
            sel = jnp.where(grp == gg, col(N_GROUPS + gg * EXPERTS_PER_GROUP + jj), sel)
        le.append(sel)
    mx = jnp.maximum(jnp.maximum(le[0], le[1]), jnp.maximum(le[2], le[3]))
    ex = [jnp.exp(v - mx) for v in le]
    chosen = []
    for jj in range(EXPERTS_PER_GROUP):
        rank = jnp.zeros(best.shape, jnp.int32)
        for ii in range(EXPERTS_PER_GROUP):
            if ii == jj:
                continue
            ahead = (ex[ii] > ex[jj]) | ((ex[ii] == ex[jj]) & (ii < jj))
            rank = rank + ahead.astype(jnp.int32)
        chosen.append(rank < 2)
    den2 = jnp.zeros(best.shape, F32)
    for jj in range(EXPERTS_PER_GROUP):
        den2 = den2 + jnp.where(chosen[jj], ex[jj], 0.0)
    lo = jnp.full(best.shape, EXPERTS_PER_GROUP, jnp.int32)
    hi = jnp.full(best.shape, -1, jnp.int32)
    for jj in range(EXPERTS_PER_GROUP):
        lo = jnp.where(chosen[jj], jnp.minimum(lo, jj), lo)
        hi = jnp.where(chosen[jj], jnp.maximum(hi, jj), hi)
    w_lo = jnp.zeros(best.shape, F32)
    w_hi = jnp.zeros(best.shape, F32)
    for jj in range(EXPERTS_PER_GROUP):
        comb = ex[jj] / den2 * gate
        w_lo = jnp.where(lo == jj, comb, w_lo)
        w_hi = jnp.where(hi == jj, comb, w_hi)
    tm = logits.shape[0]
    sub = lax.broadcasted_iota(jnp.int32, (8, tm), 0)
    base = grp * EXPERTS_PER_GROUP
    route_t = jnp.where(sub == 0, w_lo, 0.0)
    route_t = jnp.where(sub == 1, w_hi, route_t)
    route_t = jnp.where(sub == 2, (base + lo).astype(F32), route_t)
    route_t = jnp.where(sub == 3, (base + hi).astype(F32), route_t)
    route_ref[...] = jnp.concatenate([route_t, jnp.zeros((LANES - 8, tm), F32)], axis=0).T


def _outproj(x_p, x_s, om_p, om_s, or_p, or_s, w, *, tm):
    tp, d = x_p.shape
    ts = x_s.shape[0]
    npt = tp // tm
    nst = ts // tm
    p_idx = lambda i: (jnp.minimum(i, npt - 1), 0)
    s_idx = lambda i: (jnp.maximum(i - npt, 0), 0)
    mix = om_p.shape[1]
    return pl.pallas_call(
        functools.partial(_outproj_kernel, n_prompt_tiles=npt),
        grid=(npt + nst,),
        in_specs=[pl.BlockSpec((tm, d), p_idx), pl.BlockSpec((tm, d), s_idx),
                  pl.BlockSpec((tm, mix), p_idx), pl.BlockSpec((tm, mix), s_idx),
                  pl.BlockSpec((tm, mix), p_idx), pl.BlockSpec((tm, mix), s_idx),
                  _const_spec(w["w_o"].shape), _const_spec(w["g_ffn"].shape), _const_spec(w["w_r"].shape)],
        out_specs=[pl.BlockSpec((tm, d), lambda i: (i, 0)), pl.BlockSpec((tm, d), lambda i: (i, 0)),
                   pl.BlockSpec((tm, LANES), lambda i: (i, 0))],
        out_shape=[jax.ShapeDtypeStruct((tp + ts, d), F32), jax.ShapeDtypeStruct((tp + ts, d), F32),
                   jax.ShapeDtypeStruct((tp + ts, LANES), F32)],
        compiler_params=_params("parallel"), name="outproj_router",
    )(x_p, x_s, om_p, om_s, or_p, or_s, w["w_o"], w["g_ffn"], w["w_r"])


def _start_row_gather(idx_ref, base, n, src_hbm, buf, sem, unrolled=True):
    def start(r):
        pltpu.make_async_copy(src_hbm.at[pl.ds(idx_ref[base + r], 1)], buf.at[pl.ds(r, 1)], sem).start()

    if unrolled:
        for r in range(n):
            start(r)
    else:
        def body(r, carry):
            start(r)
            return carry
        lax.fori_loop(0, n, body, 0)


def _wait_row_gather(n, src_hbm, buf, sem):
    pltpu.make_async_copy(src_hbm.at[pl.ds(0, n)], buf, sem).wait()


def _moe_kernel(src_ref, te_ref, tv_ref, tf_ref, hn_hbm, wgt_ref, wut_ref, wd_ref, out_ref,
                buf0, buf1, buf2, sem, wb_ref, *, tmm):
    bufs = (buf0, buf1, buf2)
    i = pl.program_id(0)
    nt = pl.num_programs(0)
    slot = i % 3
    nxt = jnp.minimum(i + 2, nt - 1)

    @pl.when(i == 0)
    def _():
        _start_row_gather(src_ref, 0, tmm, hn_hbm, bufs[0], sem.at[0])
        _start_row_gather(src_ref, jnp.minimum(1, nt - 1) * tmm, tmm, hn_hbm, bufs[1], sem.at[1])

    @pl.when(tf_ref[i] == 1)
    def _():
        wb_ref[0] = wgt_ref[0].astype(BF16)
        wb_ref[1] = wut_ref[0].astype(BF16)
        wb_ref[2] = wd_ref[0].astype(BF16)

    for sl in range(3):
        ah = (sl + 2) % 3

        @pl.when((tv_ref[i] == 1) & (slot == sl))
        def _(sl=sl, ah=ah):
            _wait_row_gather(tmm, hn_hbm, bufs[sl], sem.at[sl])
            _start_row_gather(src_ref, nxt * tmm, tmm, hn_hbm, bufs[ah], sem.at[ah])
            hb = bufs[sl][...].astype(BF16)
            a = _silu(_mm_nt(hb, wb_ref[0])) * _mm_nt(hb, wb_ref[1])
            out_ref[...] = _mm(a.astype(BF16), wb_ref[2])

        @pl.when((tv_ref[i] == 0) & (slot == sl))
        def _(sl=sl, ah=ah):
            _wait_row_gather(tmm, hn_hbm, bufs[sl], sem.at[sl])
            _start_row_gather(src_ref, nxt * tmm, tmm, hn_hbm, bufs[ah], sem.at[ah], unrolled=False)
            out_ref[...] = jnp.zeros(out_ref.shape, F32)

        @pl.when((i == nt - 1) & (slot == sl))
        def _(sl=sl, ah=ah):
            _wait_row_gather(tmm, hn_hbm, bufs[(sl + 1) % 3], sem.at[(sl + 1) % 3])
            _wait_row_gather(tmm, hn_hbm, bufs[ah], sem.at[ah])


def _moe(src, tile_expert, tile_valid, tile_first, hn, wgt, wut, wd, *, tmm):
    n_rows = src.shape[0]
    nt = n_rows // tmm
    d = hn.shape[1]
    f = wd.shape[2]
    wspec = pl.BlockSpec((None, 1, f, d), lambda i, s, te, tv, tf: (0, te[i], 0, 0))
    grid_spec = pltpu.PrefetchScalarGridSpec(
        num_scalar_prefetch=4, grid=(nt,),
        in_specs=[pl.BlockSpec(memory_space=pl.ANY), wspec, wspec, wspec],
        out_specs=pl.BlockSpec((tmm, d), lambda i, s, te, tv, tf: (i, 0)),
        scratch_shapes=[pltpu.VMEM((tmm, d), F32), pltpu.VMEM((tmm, d), F32), pltpu.VMEM((tmm, d), F32),
                        pltpu.SemaphoreType.DMA((3,)), pltpu.VMEM((3, f, d), BF16)],
    )
    return pl.pallas_call(
        functools.partial(_moe_kernel, tmm=tmm),
        grid_spec=grid_spec, out_shape=jax.ShapeDtypeStruct((n_rows, d), F32),
        compiler_params=_params("arbitrary"), name="moe_expert_sorted",
    )(src, tile_expert, tile_valid, tile_first, hn, wgt, wut, wd)


def _final_kernel(pos_ref, x2_ref, route_ref, moe_hbm, gfin_ref, y_ref, buf, sem, *, tm, n_tok):
    i = pl.program_id(0)
    nt = pl.num_programs(0)
    slot = i % 2
    nxt = jnp.minimum(i + 1, nt - 1)

    def start(tile, sl):
        _start_row_gather(pos_ref, tile * tm, tm, moe_hbm, buf.at[sl, 0], sem.at[sl])
        _start_row_gather(pos_ref, n_tok + tile * tm, tm, moe_hbm, buf.at[sl, 1], sem.at[sl])

    def wait(sl):
        _wait_row_gather(tm, moe_hbm, buf.at[sl, 0], sem.at[sl])
        _wait_row_gather(tm, moe_hbm, buf.at[sl, 1], sem.at[sl])

    @pl.when(i == 0)
    def _():
        start(0, 0)

    start(nxt, 1 - slot)
    wait(slot)
    route = route_ref[...]
    moe = route[:, 0:1] * buf[slot, 0] + route[:, 1:2] * buf[slot, 1]
    y_ref[...] = _rms(x2_ref[...] + moe, gfin_ref[...])

    @pl.when(i == nt - 1)
    def _():
        wait(1 - slot)


def _final(pos, x2, route, moe_sorted, g_final, *, tm, tile0):
    n = pos.shape[0] // 2
    d = x2.shape[1]
    grid_spec = pltpu.PrefetchScalarGridSpec(
        num_scalar_prefetch=1, grid=(n // tm,),
        in_specs=[pl.BlockSpec((tm, d), lambda i, p: (i + tile0, 0)),
                  pl.BlockSpec((tm, LANES), lambda i, p: (i + tile0, 0)),
                  pl.BlockSpec(memory_space=pl.ANY),
                  pl.BlockSpec(g_final.shape, lambda i, p: (0, 0))],
        out_specs=pl.BlockSpec((tm, d), lambda i, p: (i, 0)),
        scratch_shapes=[pltpu.VMEM((2, 2, tm, d), F32), pltpu.SemaphoreType.DMA((2,))],
    )
    return pl.pallas_call(
        functools.partial(_final_kernel, tm=tm, n_tok=n),
        grid_spec=grid_spec, out_shape=jax.ShapeDtypeStruct((n, d), F32),
        compiler_params=_params("arbitrary"), name="final_norm",
    )(pos, x2, route, moe_sorted, g_final)


def _rope_tables(pos):
    inv = 1.0 / (ROPE_BASE ** (jnp.arange(0, ROPE, 2, dtype=F32) / ROPE))
    ang = pos.astype(F32)[:, None] * inv[None, :]
    cos = jnp.cos(ang)
    sin = jnp.sin(ang)
    return jnp.concatenate([cos] * 4, axis=1), jnp.concatenate([-sin, sin] * 2, axis=1)


def _sort_by_expert(expert, n_tok, tmm):
    n_exp = N_GROUPS * EXPERTS_PER_GROUP
    n = expert.shape[0]
    onehot = (expert[:, None] == jnp.arange(n_exp, dtype=jnp.int32)[None, :]).astype(jnp.int32)
    csum = jnp.cumsum(onehot, axis=0)
    counts = csum[-1]
    rank = jnp.take_along_axis(csum, expert[:, None], axis=1)[:, 0] - 1
    padded = ((counts + tmm - 1) // tmm) * tmm
    ends = jnp.cumsum(padded)
    pos = ((ends - padded)[expert] + rank).astype(jnp.int32)
    n_tiles = n // tmm + n_exp
    token = jnp.arange(n, dtype=jnp.int32) % n_tok
    src = jnp.zeros((n_tiles * tmm,), jnp.int32).at[pos].set(token)
    starts = jnp.arange(n_tiles, dtype=jnp.int32) * tmm
    te = jnp.sum((starts[:, None] >= ends[None, :]).astype(jnp.int32), axis=1)
    valid = (te < n_exp).astype(jnp.int32)
    te = jnp.minimum(te, n_exp - 1).astype(jnp.int32)
    first = valid * jnp.concatenate([jnp.ones((1,), jnp.int32), (te[1:] != te[:-1]).astype(jnp.int32)])
    return pos, src, te, valid, first


def kernel(x_prompt, x_sample, cache_latent, cache_kpe, state_ret, page_table, g_attn, w_in, g_q_a, w_q_b,
           g_kv_a, w_uk, w_uv, g_ret, w_o, g_ffn, w_router_group, w_router_expert, w_exp_gate, w_exp_up,
           w_exp_down, g_final):
    batch, seq, d = x_prompt.shape
    db, t_new, _ = x_sample.shape
    depth = w_in.shape[0]
    assert depth == 1, "single-layer step"
    n_pool = cache_latent.shape[1]
    past_len = page_table.shape[1] * PAGE
    tp, ts = batch * seq, db * t_new
    tm = min(TM_PROJ, seq, ts)
    assert seq % tm == 0 and ts % tm == 0 and tm % TQ_ATTN == 0 and (tp + ts) % TM_MOE == 0
    pages = min(PAGES_PER_STEP, page_table.shape[1])
    sub = min(PAGES_PER_CHAIN, pages)
    assert page_table.shape[1] % pages == 0 and pages % sub == 0
    seqs = min(RET_SAMPLE_SEQS, db)
    assert db % seqs == 0 and seq % PAGE == 0

    wi = w_in[0]
    q_a, kv_a, k_pe, rq, rk, rv, rg = (wi[:, 0:512], wi[:, 512:1024], wi[:, 1024:1088], wi[:, 1088:1600],
                                       wi[:, 1600:2112], wi[:, 2112:3136], wi[:, 3136:4160])
    w_in_r = jnp.concatenate([q_a, kv_a, rq, rk, rv, rg, k_pe, jnp.zeros((d, LANES - ROPE), F32)], axis=1)
    wqb = w_q_b[0]
    w = {
        "g_attn": g_attn[0][None, :], "w_in": w_in_r.astype(BF16),
        "g_q_a": g_q_a[0][None, :], "g_kv_a": g_kv_a[0][None, :],
        "w_qb": jnp.concatenate([wqb[:, :, :NOPE].reshape(KVL, HEADS * NOPE),
                                 jnp.pad(wqb[:, :, NOPE:], ((0, 0), (0, 0), (0, LANES - ROPE))).reshape(KVL, HEADS * LANES)],
                                axis=1).astype(BF16),
        "w_ukt": jnp.transpose(w_uk[0], (1, 2, 0)).astype(BF16),
        "w_o": w_o[0].astype(BF16), "g_ffn": g_ffn[0][None, :],
        "w_r": jnp.concatenate([w_router_group[0], w_router_expert[0],
                                jnp.zeros((d, LANES - N_GROUPS - N_GROUPS * EXPERTS_PER_GROUP), F32)],
                               axis=1).astype(BF16),
    }
    wuv = jnp.transpose(w_uv[0], (1, 0, 2)).astype(BF16)

    cs_p, sn_p = _rope_tables(jnp.arange(seq))
    cs_s, sn_s = _rope_tables(past_len + jnp.arange(t_new))
    cs_s, sn_s = jnp.tile(cs_s, (db, 1)), jnp.tile(sn_s, (db, 1))
    tiles_per_seq = seq // tm

    (c_p, kpe_p, q_p, rq_p, rk_p, rv_p, rg_p, kt, cb) = _inproj(
        x_prompt.reshape(tp, d), cs_p, sn_p, lambda i: i % tiles_per_seq, w, tm=tm, prompt=True)
    om_p = _attn_prompt(q_p, kt, cb, wuv, batch=batch, seq=seq, tq=TQ_ATTN, tk=tm)
    or_p, st_p = _retention(rq_p, rk_p, rv_p, rg_p, jnp.zeros((batch, HEADS, RET_DK, RET_DV), F32), g_ret[0],
                            seqs=1, t=PAGE, nc=seq // PAGE)

    (c_s, kpe_s, q_s, rq_s, rk_s, rv_s, rg_s) = _inproj(
        x_sample.reshape(ts, d), cs_s, sn_s, lambda i: i, w, tm=min(TM_PROJ_SAMPLE, ts), prompt=False)
    kpet_pool = jnp.swapaxes(cache_kpe.reshape(n_pool, PAGE, ROPE), 1, 2)
    o_lat_s = _attn_sample(page_table, q_s, c_s, kpe_s, cache_latent.reshape(n_pool, PAGE, KVL), kpet_pool,
                           t_new=t_new, pages=pages, sub=sub)
    om_s = _uv_sample(o_lat_s, wuv, t_new=t_new)
    or_s, st_s = _retention(rq_s, rk_s, rv_s, rg_s, state_ret[0], g_ret[0], seqs=seqs, t=t_new, nc=1)

    x2, hn, route = _outproj(x_prompt.reshape(tp, d), x_sample.reshape(ts, d), om_p, om_s, or_p, or_s, w, tm=tm)
    n_tok = tp + ts
    expert = jnp.concatenate([route[:, 2], route[:, 3]]).astype(jnp.int32)
    pos, src, tile_expert, tile_valid, tile_first = _sort_by_expert(expert, n_tok, TM_MOE)
    moe_sorted = _moe(src, tile_expert, tile_valid, tile_first, hn,
                      jnp.swapaxes(w_exp_gate, 2, 3), jnp.swapaxes(w_exp_up, 2, 3), w_exp_down, tmm=TM_MOE)
    gfin = g_final[None, :]
    tmf = min(TM_FINAL, ts)
    pos_p = jnp.concatenate([pos[:tp], pos[n_tok:n_tok + tp]])
    pos_s = jnp.concatenate([pos[tp:n_tok], pos[n_tok + tp:]])
    y_p = _final(pos_p, x2, route, moe_sorted, gfin, tm=tmf, tile0=0)
    y_s = _final(pos_s, x2, route, moe_sorted, gfin, tm=tmf, tile0=tp // tmf)

    return (y_p.reshape(batch, seq, d), y_s.reshape(db, t_new, d),
            c_p.reshape(1, batch, seq, KVL), kpe_p.reshape(1, batch, seq, ROPE), st_p[None],
            c_s.reshape(1, db, t_new, KVL), kpe_s.reshape(1, db, t_new, ROPE), st_s[None])
```

```python
import functools

import jax
import jax.numpy as jnp
from jax import lax
from jax.experimental import pallas as pl
from jax.experimental.pallas import tpu as pltpu

F32 = jnp.float32
BF16 = jnp.bfloat16

HEADS = 8
NOPE = 128
ROPE = 64
KVL = 512
VDIM = 128
RET_DK = 64
RET_DV = 128
PAGE = 128
N_GROUPS = 4
EXPERTS_PER_GROUP = 4
EPS = 1e-6
ROPE_BASE = 10000.0
NEG = -1e30
LOG2E = 1.4426950408889634
Q_SCALE = (NOPE + ROPE) ** -0.5 * LOG2E
QK_W = KVL + 128

V7X_VMEM_BYTES = 64 * 1024 * 1024
VMEM_LIMIT = V7X_VMEM_BYTES - 4 * 1024 * 1024
LANES = 128

TM_PROJ = 512
TM_PROJ_SAMPLE = 256
TQ_ATTN = 256
PAGES_PER_STEP = 32
PAGES_PER_CHAIN = 32
RET_SAMPLE_SEQS = 16
TM_MOE = 256
TM_FINAL = 256


def _mm(a, b):
    return jnp.dot(a, b, preferred_element_type=F32)


def _mm_nt(a, b):
    return lax.dot_general(a, b, (((1,), (1,)), ((), ())), preferred_element_type=F32)


def _mm_tn(a, b):
    return lax.dot_general(a, b, (((0,), (0,)), ((), ())), preferred_element_type=F32)


def _rms(x, g):
    return x * lax.rsqrt(jnp.mean(x * x, axis=-1, keepdims=True) + EPS) * g


def _silu(x):
    return x * (1.0 / (1.0 + jnp.exp(-x)))


def _rope(x, cos, sin):
    w = x.shape[-1]
    lane = lax.broadcasted_iota(jnp.int32, x.shape, 1)
    first_half = (lane % ROPE) < (ROPE // 2)
    swapped = jnp.where(first_half, pltpu.roll(x, w - ROPE // 2, 1), pltpu.roll(x, ROPE // 2, 1))
    return x * cos + swapped * sin


def _const_spec(shape):
    nd = len(shape)
    return pl.BlockSpec(shape, lambda *_: (0,) * nd, pipeline_mode=pl.Buffered(1))


def _params(*sem):
    return pltpu.CompilerParams(dimension_semantics=sem, vmem_limit_bytes=VMEM_LIMIT)


def _inproj_kernel(x_ref, cs_ref, sn_ref, gattn_ref, win_ref, gq_ref, gkv_ref, wqb_ref, wukt_ref,
                   c_ref, kpe_ref, q_ref, rq_ref, rk_ref, rv_ref, rg_ref, *rest, prompt):
    h = _rms(x_ref[...], gattn_ref[...]).astype(BF16)
    cs = cs_ref[...]
    sn = sn_ref[...]
    cs4 = jnp.concatenate([cs] * 4, axis=1)
    sn4 = jnp.concatenate([sn] * 4, axis=1)
    cs8 = jnp.concatenate([cs4] * 2, axis=1)
    sn8 = jnp.concatenate([sn4] * 2, axis=1)

    def proj(lo, hi):
        return _mm(h, win_ref[:, lo:hi])

    c = _rms(proj(512, 1024), gkv_ref[...])
    c_ref[...] = c
    kpe128 = _rope(proj(4096, 4224), cs, sn)
    kpe_ref[...] = kpe128[:, :ROPE]

    qn = _rms(proj(0, 512), gq_ref[...]).astype(BF16)
    q = _mm(qn, wqb_ref[...])
    q_pe = _rope(q[:, HEADS * NOPE:], cs8, sn8) * Q_SCALE
    for hh in range(HEADS):
        q_nope = q[:, hh * NOPE:(hh + 1) * NOPE].astype(BF16)
        q_lat = _mm(q_nope, wukt_ref[hh]) * Q_SCALE
        q_ref[hh] = jnp.concatenate([q_lat, q_pe[:, hh * LANES:(hh + 1) * LANES]], axis=1).astype(q_ref.dtype)

    rq_ref[...] = (_rope(proj(1024, 1536), cs4, sn4) * (RET_DK ** -0.5)).astype(rq_ref.dtype)
    rk_ref[...] = _rope(proj(1536, 2048), cs4, sn4).astype(rk_ref.dtype)
    rv_ref[...] = proj(2048, 3072).astype(rv_ref.dtype)
    rg_ref[...] = _silu(proj(3072, 4096)).astype(rg_ref.dtype)

    if prompt:
        kt_ref, cb_ref = rest
        cb_ref[...] = c.astype(BF16)
        kt_ref[0] = jnp.concatenate([c.T, kpe128.T], axis=0).astype(BF16)


def _inproj(x2d, cs, sn, cs_index, w, *, tm, prompt):
    t_tok, d = x2d.shape
    nt = t_tok // tm
    adt = BF16 if prompt else F32
    row = lambda width: pl.BlockSpec((tm, width), lambda i: (i, 0))
    out_shape = [
        jax.ShapeDtypeStruct((t_tok, KVL), F32), jax.ShapeDtypeStruct((t_tok, ROPE), F32),
        jax.ShapeDtypeStruct((HEADS, t_tok, QK_W), adt),
        jax.ShapeDtypeStruct((t_tok, HEADS * RET_DK), adt), jax.ShapeDtypeStruct((t_tok, HEADS * RET_DK), adt),
        jax.ShapeDtypeStruct((t_tok, HEADS * RET_DV), adt), jax.ShapeDtypeStruct((t_tok, HEADS * RET_DV), adt),
    ]
    out_specs = [
        row(KVL), row(ROPE),
        pl.BlockSpec((HEADS, tm, QK_W), lambda i: (0, i, 0)),
        row(HEADS * RET_DK), row(HEADS * RET_DK), row(HEADS * RET_DV), row(HEADS * RET_DV),
    ]
    if prompt:
        out_shape += [jax.ShapeDtypeStruct((nt, QK_W, tm), BF16), jax.ShapeDtypeStruct((t_tok, KVL), BF16)]
        out_specs += [pl.BlockSpec((1, QK_W, tm), lambda i: (i, 0, 0)), row(KVL)]
    in_specs = [
        row(d),
        pl.BlockSpec((tm, LANES), lambda i: (cs_index(i), 0)), pl.BlockSpec((tm, LANES), lambda i: (cs_index(i), 0)),
        _const_spec(w["g_attn"].shape), _const_spec(w["w_in"].shape), _const_spec(w["g_q_a"].shape),
        _const_spec(w["g_kv_a"].shape), _const_spec(w["w_qb"].shape), _const_spec(w["w_ukt"].shape),
    ]
    return pl.pallas_call(
        functools.partial(_inproj_kernel, prompt=prompt),
        grid=(nt,), in_specs=in_specs, out_specs=out_specs, out_shape=out_shape,
        compiler_params=_params("parallel"), name="inproj_prompt" if prompt else "inproj_sample",
    )(x2d, cs, sn, w["g_attn"], w["w_in"], w["g_q_a"], w["g_kv_a"], w["w_qb"], w["w_ukt"])


def _attn_prompt_kernel(q_ref, kt_ref, cb_ref, wuv_ref, o_ref, m_ref, l_ref, acc_ref, sa_ref, sb_ref, *, tq, tk):
    qi = pl.program_id(1)
    rows = HEADS * tq
    q = q_ref[...].reshape(rows, QK_W)
    m_ref[...] = jnp.full((rows, 1), NEG, F32)
    l_ref[...] = jnp.zeros((rows, 1), F32)
    acc_ref[...] = jnp.zeros((rows, KVL), F32)

    def scores(j, s_ref):
        s_ref[...] = _mm(q, kt_ref[j])

    def consume(s_ref, j, masked):
        s = s_ref[...]
        if masked:
            q_pos = lax.broadcasted_iota(jnp.int32, (rows, tk), 0) % tq + qi * tq
            k_pos = lax.broadcasted_iota(jnp.int32, (rows, tk), 1) + j * tk
            s = jnp.where(k_pos <= q_pos, s, NEG)
        m_prev = m_ref[...]
        m_new = jnp.maximum(m_prev, jnp.max(s, axis=1, keepdims=True))
        alpha = jnp.exp2(m_prev - m_new)
        p = jnp.exp2(s - m_new)
        l_ref[...] = alpha * l_ref[...] + jnp.sum(p, axis=1, keepdims=True)
        v = cb_ref[pl.ds(pl.multiple_of(j * tk, tk), tk), :]
        acc_ref[...] = alpha * acc_ref[...] + _mm(p.astype(BF16), v)
        m_ref[...] = m_new

    last = (qi * tq) // tk
    scores(0, sa_ref)

    def pair(i, carry):
        j = 2 * i
        scores(j + 1, sb_ref)
        consume(sa_ref, j, False)
        scores(j + 2, sa_ref)
        consume(sb_ref, j + 1, False)
        return carry

    lax.fori_loop(0, last // 2, pair, 0)

    @pl.when(last % 2 == 0)
    def _():
        consume(sa_ref, last, True)

    @pl.when(last % 2 == 1)
    def _():
        scores(last, sb_ref)
        consume(sa_ref, last - 1, False)
        consume(sb_ref, last, True)

    o = acc_ref[...] / l_ref[...]
    for hh in range(HEADS):
        oh = o[hh * tq:(hh + 1) * tq].astype(BF16)
        o_ref[:, hh * VDIM:(hh + 1) * VDIM] = _mm(oh, wuv_ref[hh]).astype(o_ref.dtype)


def _attn_prompt(q, kt, cb, wuv, *, batch, seq, tq, tk):
    nq = seq // tq
    nkb = seq // tk
    rows = HEADS * tq
    return pl.pallas_call(
        functools.partial(_attn_prompt_kernel, tq=tq, tk=tk),
        grid=(batch, nq),
        in_specs=[
            pl.BlockSpec((HEADS, tq, QK_W), lambda b, i: (0, b * nq + i, 0)),
            pl.BlockSpec((nkb, QK_W, tk), lambda b, i: (b, 0, 0)),
            pl.BlockSpec((seq, KVL), lambda b, i: (b, 0)),
            _const_spec(wuv.shape),
        ],
        out_specs=pl.BlockSpec((tq, HEADS * VDIM), lambda b, i: (b * nq + i, 0)),
        out_shape=jax.ShapeDtypeStruct((batch * seq, HEADS * VDIM), BF16),
        scratch_shapes=[pltpu.VMEM((rows, 1), F32), pltpu.VMEM((rows, 1), F32), pltpu.VMEM((rows, KVL), F32),
                        pltpu.VMEM((rows, tk), F32), pltpu.VMEM((rows, tk), F32)],
        compiler_params=_params("parallel", "arbitrary"), name="attn_prompt",
    )(q, kt, cb, wuv)


def _attn_sample_kernel(pt_ref, q_ref, cnew_ref, knew_ref, lat_hbm, kpet_hbm, o_ref,
                        cbuf0, cbuf1, cbuf2, kbuf0, kbuf1, kbuf2, sem, qt_ref, qpe_ref, m_ref, l_ref, acc_ref,
                        *, pages, sub, nblk, t_new):
    cbufs = (cbuf0, cbuf1, cbuf2)
    kbufs = (kbuf0, kbuf1, kbuf2)
    b = pl.program_id(0)
    j = pl.program_id(1)
    step = b * nblk + j
    total = pl.num_programs(0) * nblk
    slot = step % 3
    rows = HEADS * t_new

    def start_step_pages(s, sl):
        bb = s // nblk
        jj = s % nblk
        for p in range(pages):
            page = pt_ref[bb, jj * pages + p]
            pltpu.make_async_copy(lat_hbm.at[page], cbufs[sl].at[p], sem.at[0, sl]).start()
            pltpu.make_async_copy(kpet_hbm.at[page], kbufs[sl].at[p], sem.at[1, sl]).start()

    def wait_pages(sl):
        pltpu.make_async_copy(lat_hbm.at[pl.ds(0, pages)], cbufs[sl], sem.at[0, sl]).wait()
        pltpu.make_async_copy(kpet_hbm.at[pl.ds(0, pages)], kbufs[sl], sem.at[1, sl]).wait()

    @pl.when(step == 0)
    def _():
        start_step_pages(0, 0)
        start_step_pages(jnp.minimum(1, total - 1), 1)

    @pl.when(j == 0)
    def _():
        q = q_ref[...].reshape(rows, QK_W)
        zpad = jnp.zeros((LANES - rows, QK_W), F32)
        qt_ref[...] = jnp.concatenate([q, zpad], axis=0).T.astype(BF16)
        qpe_ref[...] = q[:, KVL:KVL + ROPE].astype(BF16)
        m_ref[...] = jnp.full((rows, 1), NEG, F32)
        l_ref[...] = jnp.zeros((rows, 1), F32)
        acc_ref[...] = jnp.zeros((rows, KVL), F32)

    def partial_softmax(s, v):
        m_c = jnp.max(s, axis=1, keepdims=True)
        p = jnp.exp2(s - m_c)
        return m_c, jnp.sum(p, axis=1, keepdims=True), _mm(p.astype(BF16), v)

    def merge(parts):
        m_prev = m_ref[...]
        m_new = m_prev
        for m_c, _, _ in parts:
            m_new = jnp.maximum(m_new, m_c)
        alpha = jnp.exp2(m_prev - m_new)
        l_new = alpha * l_ref[...]
        acc = alpha * acc_ref[...]
        for m_c, l_c, a_c in parts:
            w_c = jnp.exp2(m_c - m_new)
            l_new = l_new + w_c * l_c
            acc = acc + w_c * a_c
        m_ref[...] = m_new
        l_ref[...] = l_new
        acc_ref[...] = acc

    nxt = jnp.minimum(step + 2, total - 1)
    for sl in range(3):
        @pl.when(slot == sl)
        def _(sl=sl):
            wait_pages(sl)
            start_step_pages(nxt, (sl + 2) % 3)
            q_lat_t = qt_ref[0:KVL, :]
            q_pe = qpe_ref[...]
            parts = []
            for c0 in range(0, pages, sub):
                kc = cbufs[sl][c0:c0 + sub].reshape(sub * PAGE, KVL).astype(BF16)
                kpt = jnp.concatenate([kbufs[sl][c0 + p] for p in range(sub)], axis=1).astype(BF16)
                parts.append(partial_softmax(_mm(kc, q_lat_t).T[:rows] + _mm(q_pe, kpt), kc))
            merge(parts)

    @pl.when(j == nblk - 1)
    def _():
        q_lat_t = qt_ref[0:KVL, :]
        q_pe = qpe_ref[...]
        zc = jnp.zeros((LANES - t_new, KVL), F32)
        zk = jnp.zeros((LANES - t_new, ROPE), F32)
        cn = jnp.concatenate([cnew_ref[...], zc], axis=0).astype(BF16)
        kn = jnp.concatenate([knew_ref[...], zk], axis=0).astype(BF16)
        s2 = _mm(cn, q_lat_t).T[:rows] + _mm_nt(q_pe, kn)
        q_t = lax.broadcasted_iota(jnp.int32, (rows, LANES), 0) % t_new
        k_t = lax.broadcasted_iota(jnp.int32, (rows, LANES), 1)
        merge([partial_softmax(jnp.where(k_t <= q_t, s2, NEG), cn)])
        o_ref[0] = acc_ref[...] / l_ref[...]

    for sl in range(3):
        @pl.when((step == total - 1) & (slot == sl))
        def _(sl=sl):
            wait_pages((sl + 1) % 3)
            wait_pages((sl + 2) % 3)


def _attn_sample(page_table, q, c_new, kpe_new, lat_pool, kpet_pool, *, t_new, pages, sub):
    db, n_pages = page_table.shape
    nblk = n_pages // pages
    rows = HEADS * t_new
    grid_spec = pltpu.PrefetchScalarGridSpec(
        num_scalar_prefetch=1, grid=(db, nblk),
        in_specs=[
            pl.BlockSpec((HEADS, t_new, QK_W), lambda b, j, pt: (0, b, 0)),
            pl.BlockSpec((t_new, KVL), lambda b, j, pt: (b, 0)),
            pl.BlockSpec((t_new, ROPE), lambda b, j, pt: (b, 0)),
            pl.BlockSpec(memory_space=pl.ANY), pl.BlockSpec(memory_space=pl.ANY),
        ],
        out_specs=pl.BlockSpec((1, rows, KVL), lambda b, j, pt: (b, 0, 0)),
        scratch_shapes=[
            pltpu.VMEM((pages, PAGE, KVL), F32), pltpu.VMEM((pages, PAGE, KVL), F32),
            pltpu.VMEM((pages, PAGE, KVL), F32),
            pltpu.VMEM((pages, ROPE, PAGE), F32), pltpu.VMEM((pages, ROPE, PAGE), F32),
            pltpu.VMEM((pages, ROPE, PAGE), F32),
            pltpu.SemaphoreType.DMA((2, 3)),
            pltpu.VMEM((QK_W, LANES), BF16), pltpu.VMEM((rows, ROPE), BF16),
            pltpu.VMEM((rows, 1), F32), pltpu.VMEM((rows, 1), F32), pltpu.VMEM((rows, KVL), F32),
        ],
    )
    return pl.pallas_call(
        functools.partial(_attn_sample_kernel, pages=pages, sub=sub, nblk=nblk, t_new=t_new),
        grid_spec=grid_spec, out_shape=jax.ShapeDtypeStruct((db, rows, KVL), F32),
        compiler_params=_params("arbitrary", "arbitrary"), name="attn_sample",
    )(page_table, q, c_new, kpe_new, lat_pool, kpet_pool)


def _uv_sample_kernel(o_ref, wuv_ref, out_ref):
    n, t_new, _ = o_ref.shape
    out_ref[...] = _mm(o_ref[...].reshape(n * t_new, KVL).astype(BF16), wuv_ref[0]).astype(out_ref.dtype)


def _uv_sample(o_lat, wuv, *, t_new):
    db = o_lat.shape[0]
    o4 = o_lat.reshape(db, HEADS, t_new, KVL)
    return pl.pallas_call(
        _uv_sample_kernel, grid=(HEADS,),
        in_specs=[pl.BlockSpec((db, None, t_new, KVL), lambda h: (0, h, 0, 0)),
                  pl.BlockSpec((1, KVL, VDIM), lambda h: (h, 0, 0))],
        out_specs=pl.BlockSpec((db * t_new, VDIM), lambda h: (0, h)),
        out_shape=jax.ShapeDtypeStruct((db * t_new, HEADS * VDIM), BF16),
        compiler_params=_params("parallel"), name="uv_sample",
    )(o4, wuv)


def _ret_kernel(rq_ref, rk_ref, rv_ref, rg_ref, s0_ref, di_ref, dq_ref, dk_ref, ds_ref, gr_ref,
                o_ref, sf_ref, st_ref, *, seqs, t, nc):
    c = pl.program_id(1)

    @pl.when(c == 0)
    def _():
        st_ref[...] = s0_ref[...]

    q = rq_ref[...]
    k = rk_ref[...]
    v = rv_ref[...]
    outs = []
    for hh in range(HEADS):
        qh = q[:, hh * RET_DK:(hh + 1) * RET_DK]
        kh = k[:, hh * RET_DK:(hh + 1) * RET_DK]
        vh = v[:, hh * RET_DV:(hh + 1) * RET_DV]
        a = _mm_nt(qh.astype(BF16), kh.astype(BF16)) * di_ref[hh]
        o = _mm(a.astype(BF16), vh.astype(BF16))
        kd = kh.astype(F32) * dk_ref[hh]
        cross = []
        for g in range(seqs):
            sl = slice(g * t, (g + 1) * t)
            state = st_ref[g, hh]
            cross.append(_mm(qh[sl].astype(BF16), state.astype(BF16)))
            st_ref[g, hh] = state * ds_ref[hh] + _mm_tn(kd[sl].astype(BF16), vh[sl].astype(BF16))
        o = o + (cross[0] if seqs == 1 else jnp.concatenate(cross, axis=0)) * dq_ref[hh]
        outs.append(_rms(o, gr_ref[hh:hh + 1, :]))
    o_ref[...] = (rg_ref[...].astype(F32) * jnp.concatenate(outs, axis=1)).astype(o_ref.dtype)

    @pl.when(c == nc - 1)
    def _():
        sf_ref[...] = st_ref[...]


def _retention(rq, rk, rv, rg, s0, g_ret, *, seqs, t, nc):
    rows = seqs * t
    n_state = s0.shape[0]
    nb = n_state // seqs
    hh = jnp.arange(HEADS, dtype=F32)
    log_g = jnp.log1p(-jnp.exp2(-5.0 - hh))
    i = jnp.arange(t, dtype=F32)
    diff = i[:, None] - i[None, :]
    d_intra = jnp.where(diff >= 0, jnp.exp(jnp.maximum(diff, 0.0)[None] * log_g[:, None, None]), 0.0)
    d_q = jnp.exp((i[None, :] + 1.0) * log_g[:, None])
    d_k = jnp.exp((t - 1.0 - i)[None, :] * log_g[:, None])
    d_s = jnp.exp(t * log_g)
    eye = jnp.eye(seqs, dtype=F32)
    di = jnp.einsum("gk,hij->hgikj", eye, d_intra).reshape(HEADS, rows, rows)
    dq = jnp.broadcast_to(jnp.tile(d_q, (1, seqs))[:, :, None], (HEADS, rows, RET_DV))
    dk = jnp.broadcast_to(jnp.tile(d_k, (1, seqs))[:, :, None], (HEADS, rows, RET_DK))
    ds = jnp.broadcast_to(d_s[:, None, None], (HEADS, RET_DK, RET_DV))
    blk = lambda width: pl.BlockSpec((rows, width), lambda s, c: (s * nc + c, 0))
    state_spec = pl.BlockSpec((seqs, HEADS, RET_DK, RET_DV), lambda s, c: (s, 0, 0, 0))
    return pl.pallas_call(
        functools.partial(_ret_kernel, seqs=seqs, t=t, nc=nc),
        grid=(nb, nc),
        in_specs=[blk(HEADS * RET_DK), blk(HEADS * RET_DK), blk(HEADS * RET_DV), blk(HEADS * RET_DV), state_spec,
                  _const_spec(di.shape), _const_spec(dq.shape), _const_spec(dk.shape), _const_spec(ds.shape),
                  _const_spec(g_ret.shape)],
        out_specs=[blk(HEADS * RET_DV), state_spec],
        out_shape=[jax.ShapeDtypeStruct((nb * nc * rows, HEADS * RET_DV), BF16),
                   jax.ShapeDtypeStruct(s0.shape, F32)],
        scratch_shapes=[pltpu.VMEM((seqs, HEADS, RET_DK, RET_DV), F32)],
        compiler_params=_params("parallel", "arbitrary"), name="retention_t%d" % t,
    )(rq, rk, rv, rg, s0, di, dq, dk, ds, g_ret)


def _outproj_kernel(xp_ref, xs_ref, omp_ref, oms_ref, orp_ref, ors_ref, wo_ref, gffn_ref, wr_ref,
                    x2_ref, hn_ref, route_ref, *, n_prompt_tiles):
    from_prompt = pl.program_id(0) < n_prompt_tiles
    x = jnp.where(from_prompt, xp_ref[...], xs_ref[...])
    o_mla = jnp.where(from_prompt, omp_ref[...], oms_ref[...])
    o_ret = jnp.where(from_prompt, orp_ref[...], ors_ref[...])
    half = HEADS * VDIM
    x2 = x + _mm(o_mla, wo_ref[0:half, :]) + _mm(o_ret, wo_ref[half:, :])
    x2_ref[...] = x2
    hn = _rms(x2, gffn_ref[...])
    hn_ref[...] = hn
    logits = _mm(hn.astype(BF16), wr_ref[...])
    logits_t = logits.T

    def col(kk):
        return logits_t[kk:kk + 1, :]

    best = col(0)
    grp = jnp.zeros(best.shape, jnp.int32)
    for kk in range(1, N_GROUPS):
        upd = col(kk) > best
        grp = jnp.where(upd, kk, grp)
        best = jnp.where(upd, col(kk), best)
    den = jnp.exp(col(0) - best)
    for kk in range(1, N_GROUPS):
        den = den + jnp.exp(col(kk) - best)
    gate = 1.0 / den

    le = []
    for jj in range(EXPERTS_PER_GROUP):
        sel = col(N_GROUPS + (N_GROUPS - 1) * EXPERTS_PER_GROUP + jj)
        for gg in range(N_GROUPS - 2, -1, -1):
            sel = jnp.where(grp == gg, col(N_GROUPS + gg * EXPERTS_PER_GROUP + jj), sel)
        le.append(sel)
    mx = jnp.maximum(jnp.maximum(le[0], le[1]), jnp.maximum(le[2], le[3]))
    ex = [jnp.exp(v - mx) for v in le]
    chosen = []
    for jj in range(EXPERTS_PER_GROUP):
        rank = jnp.zeros(best.shape, jnp.int32)
        for ii in range(EXPERTS_PER_GROUP):
            if ii == jj:
                continue
            ahead = (ex[ii] > ex[jj]) | ((ex[ii] == ex[jj]) & (ii < jj))
            rank = rank + ahead.astype(jnp.int32)
        chosen.append(rank < 2)
    den2 = jnp.zeros(best.shape, F32)
    for jj in range(EXPERTS_PER_GROUP):
        den2 = den2 + jnp.where(chosen[jj], ex[jj], 0.0)
    lo = jnp.full(best.shape, EXPERTS_PER_GROUP, jnp.int32)
    hi = jnp.full(best.shape, -1, jnp.int32)
    for jj in range(EXPERTS_PER_GROUP):
        lo = jnp.where(chosen[jj], jnp.minimum(lo, jj), lo)
        hi = jnp.where(chosen[jj], jnp.maximum(hi, jj), hi)
    w_lo = jnp.zeros(best.shape, F32)
    w_hi = jnp.zeros(best.shape, F32)
    for jj in range(EXPERTS_PER_GROUP):
        comb = ex[jj] / den2 * gate
        w_lo = jnp.where(lo == jj, comb, w_lo)
        w_hi = jnp.where(hi == jj, comb, w_hi)
    tm = logits.shape[0]
    sub = lax.broadcasted_iota(jnp.int32, (8, tm), 0)
    base = grp * EXPERTS_PER_GROUP
    route_t = jnp.where(sub == 0, w_lo, 0.0)
    route_t = jnp.where(sub == 1, w_hi, route_t)
    route_t = jnp.where(sub == 2, (base + lo).astype(F32), route_t)
    route_t = jnp.where(sub == 3, (base + hi).astype(F32), route_t)
    route_ref[...] = jnp.concatenate([route_t, jnp.zeros((LANES - 8, tm), F32)], axis=0).T


def _outproj(x_p, x_s, om_p, om_s, or_p, or_s, w, *, tm):
    tp, d = x_p.shape
    ts = x_s.shape[0]
    npt = tp // tm
    nst = ts // tm
    p_idx = lambda i: (jnp.minimum(i, npt - 1), 0)
    s_idx = lambda i: (jnp.maximum(i - npt, 0), 0)
    mix = om_p.shape[1]
    return pl.pallas_call(
        functools.partial(_outproj_kernel, n_prompt_tiles=npt),
        grid=(npt + nst,),
        in_specs=[pl.BlockSpec((tm, d), p_idx), pl.BlockSpec((tm, d), s_idx),
                  pl.BlockSpec((tm, mix), p_idx), pl.BlockSpec((tm, mix), s_idx),
                  pl.BlockSpec((tm, mix), p_idx), pl.BlockSpec((tm, mix), s_idx),
                  _const_spec(w["w_o"].shape), _const_spec(w["g_ffn"].shape), _const_spec(w["w_r"].shape)],
        out_specs=[pl.BlockSpec((tm, d), lambda i: (i, 0)), pl.BlockSpec((tm, d), lambda i: (i, 0)),
                   pl.BlockSpec((tm, LANES), lambda i: (i, 0))],
        out_shape=[jax.ShapeDtypeStruct((tp + ts, d), F32), jax.ShapeDtypeStruct((tp + ts, d), F32),
                   jax.ShapeDtypeStruct((tp + ts, LANES), F32)],
        compiler_params=_params("parallel"), name="outproj_router",
    )(x_p, x_s, om_p, om_s, or_p, or_s, w["w_o"], w["g_ffn"], w["w_r"])


def _start_row_gather(idx_ref, base, n, src_hbm, buf, sem):
    for r in range(n):
        pltpu.make_async_copy(src_hbm.at[pl.ds(idx_ref[base + r], 1)], buf.at[pl.ds(r, 1)], sem).start()


def _wait_row_gather(n, src_hbm, buf, sem):
    pltpu.make_async_copy(src_hbm.at[pl.ds(0, n)], buf, sem).wait()


def _moe_kernel(src_ref, te_ref, tv_ref, tf_ref, nv_ref, hn_hbm, wgt_ref, wut_ref, wd_ref, out_ref,
                buf0, buf1, buf2, sem, wb_ref, *, tmm):
    bufs = (buf0, buf1, buf2)
    i = pl.program_id(0)
    nv = nv_ref[0]
    slot = i % 3
    nxt = jnp.minimum(i + 2, nv - 1)

    def start(base, buf, s):
        _start_row_gather(src_ref, base, tmm, hn_hbm, buf, s)

    def wait(buf, s):
        _wait_row_gather(tmm, hn_hbm, buf, s)

    @pl.when(i == 0)
    def _():
        start(0, bufs[0], sem.at[0])
        start(jnp.minimum(1, nv - 1) * tmm, bufs[1], sem.at[1])

    @pl.when(tf_ref[i] == 1)
    def _():
        wb_ref[0] = wgt_ref[0].astype(BF16)
        wb_ref[1] = wut_ref[0].astype(BF16)
        wb_ref[2] = wd_ref[0].astype(BF16)

    for sl in range(3):
        ah = (sl + 2) % 3

        @pl.when((tv_ref[i] == 1) & (slot == sl))
        def _(sl=sl, ah=ah):
            wait(bufs[sl], sem.at[sl])
            start(nxt * tmm, bufs[ah], sem.at[ah])
            hb = bufs[sl][...].astype(BF16)
            a = _silu(_mm_nt(hb, wb_ref[0])) * _mm_nt(hb, wb_ref[1])
            out_ref[...] = _mm(a.astype(BF16), wb_ref[2])

        @pl.when((i == nv - 1) & (slot == sl))
        def _(sl=sl, ah=ah):
            wait(bufs[(sl + 1) % 3], sem.at[(sl + 1) % 3])
            wait(bufs[ah], sem.at[ah])

    @pl.when(tv_ref[i] == 0)
    def _():
        out_ref[...] = jnp.zeros(out_ref.shape, F32)


def _moe(src, tile_expert, tile_valid, tile_first, hn, wgt, wut, wd, *, tmm):
    n_rows = src.shape[0]
    nt = n_rows // tmm
    d = hn.shape[1]
    f = wd.shape[2]
    wspec = pl.BlockSpec((None, 1, f, d), lambda i, s, te, tv, tf, nv: (0, te[i], 0, 0))
    row_buf = pltpu.VMEM((tmm, d), F32)
    n_valid = jnp.sum(tile_valid, keepdims=True).astype(jnp.int32)
    grid_spec = pltpu.PrefetchScalarGridSpec(
        num_scalar_prefetch=5, grid=(nt,),
        in_specs=[pl.BlockSpec(memory_space=pl.ANY), wspec, wspec, wspec],
        out_specs=pl.BlockSpec((tmm, d), lambda i, s, te, tv, tf, nv: (i, 0)),
        scratch_shapes=[row_buf, row_buf, row_buf,
                        pltpu.SemaphoreType.DMA((3,)), pltpu.VMEM((3, f, d), BF16)],
    )
    return pl.pallas_call(
        functools.partial(_moe_kernel, tmm=tmm),
        grid_spec=grid_spec, out_shape=jax.ShapeDtypeStruct((n_rows, d), F32),
        compiler_params=_params("arbitrary"), name="moe_expert_sorted",
    )(src, tile_expert, tile_valid, tile_first, n_valid, hn, wgt, wut, wd)


def _final_kernel(pos_ref, x2_ref, route_ref, moe_hbm, gfin_ref, y_ref, buf, sem, *, tm, n_tok):
    i = pl.program_id(0)
    nt = pl.num_programs(0)
    slot = i % 2
    nxt = jnp.minimum(i + 1, nt - 1)

    def start(tile, sl):
        _start_row_gather(pos_ref, tile * tm, tm, moe_hbm, buf.at[sl, 0], sem.at[sl])
        _start_row_gather(pos_ref, n_tok + tile * tm, tm, moe_hbm, buf.at[sl, 1], sem.at[sl])

    def wait(sl):
        _wait_row_gather(tm, moe_hbm, buf.at[sl, 0], sem.at[sl])
        _wait_row_gather(tm, moe_hbm, buf.at[sl, 1], sem.at[sl])

    @pl.when(i == 0)
    def _():
        start(0, 0)

    start(nxt, 1 - slot)
    wait(slot)
    route = route_ref[...]
    moe = route[:, 0:1] * buf[slot, 0] + route[:, 1:2] * buf[slot, 1]
    y_ref[...] = _rms(x2_ref[...] + moe, gfin_ref[...])

    @pl.when(i == nt - 1)
    def _():
        wait(1 - slot)


def _final(pos, x2, route, moe_sorted, g_final, *, tm, tile0):
    n = pos.shape[0] // 2
    d = x2.shape[1]
    grid_spec = pltpu.PrefetchScalarGridSpec(
        num_scalar_prefetch=1, grid=(n // tm,),
        in_specs=[pl.BlockSpec((tm, d), lambda i, p: (i + tile0, 0)),
                  pl.BlockSpec((tm, LANES), lambda i, p: (i + tile0, 0)),
                  pl.BlockSpec(memory_space=pl.ANY),
                  pl.BlockSpec(g_final.shape, lambda i, p: (0, 0))],
        out_specs=pl.BlockSpec((tm, d), lambda i, p: (i, 0)),
        scratch_shapes=[pltpu.VMEM((2, 2, tm, d), F32), pltpu.SemaphoreType.DMA((2,))],
    )
    return pl.pallas_call(
        functools.partial(_final_kernel, tm=tm, n_tok=n),
        grid_spec=grid_spec, out_shape=jax.ShapeDtypeStruct((n, d), F32),
        compiler_params=_params("arbitrary"), name="final_norm",
    )(pos, x2, route, moe_sorted, g_final)


def _rope_tables(pos):
    inv = 1.0 / (ROPE_BASE ** (jnp.arange(0, ROPE, 2, dtype=F32) / ROPE))
    ang = pos.astype(F32)[:, None] * inv[None, :]
    cos = jnp.cos(ang)
    sin = jnp.sin(ang)
    return jnp.concatenate([cos] * 4, axis=1), jnp.concatenate([-sin, sin] * 2, axis=1)


def _sort_by_expert(expert, n_tok, tmm):
    n_exp = N_GROUPS * EXPERTS_PER_GROUP
    n = expert.shape[0]
    onehot = (expert[:, None] == jnp.arange(n_exp, dtype=jnp.int32)[None, :]).astype(jnp.int32)
    csum = jnp.cumsum(onehot, axis=0)
    counts = csum[-1]
    rank = jnp.take_along_axis(csum, expert[:, None], axis=1)[:, 0] - 1
    padded = ((counts + tmm - 1) // tmm) * tmm
    ends = jnp.cumsum(padded)
    pos = ((ends - padded)[expert] + rank).astype(jnp.int32)
    n_tiles = n // tmm + n_exp
    token = jnp.arange(n, dtype=jnp.int32) % n_tok
    src = jnp.zeros((n_tiles * tmm,), jnp.int32).at[pos].set(token)
    starts = jnp.arange(n_tiles, dtype=jnp.int32) * tmm
    te = jnp.sum((starts[:, None] >= ends[None, :]).astype(jnp.int32), axis=1)
    valid = (te < n_exp).astype(jnp.int32)
    te = jnp.minimum(te, n_exp - 1).astype(jnp.int32)
    first = valid * jnp.concatenate([jnp.ones((1,), jnp.int32), (te[1:] != te[:-1]).astype(jnp.int32)])
    return pos, src, te, valid, first


def kernel(x_prompt, x_sample, cache_latent, cache_kpe, state_ret, page_table, g_attn, w_in, g_q_a, w_q_b,
           g_kv_a, w_uk, w_uv, g_ret, w_o, g_ffn, w_router_group, w_router_expert, w_exp_gate, w_exp_up,
           w_exp_down, g_final):
    batch, seq, d = x_prompt.shape
    db, t_new, _ = x_sample.shape
    depth = w_in.shape[0]
    assert depth == 1, "single-layer step"
    n_pool = cache_latent.shape[1]
    past_len = page_table.shape[1] * PAGE
    tp, ts = batch * seq, db * t_new
    tm = min(TM_PROJ, seq, ts)
    assert seq % tm == 0 and ts % tm == 0 and tm % TQ_ATTN == 0 and (tp + ts) % TM_MOE == 0
    pages = min(PAGES_PER_STEP, page_table.shape[1])
    sub = min(PAGES_PER_CHAIN, pages)
    assert page_table.shape[1] % pages == 0 and pages % sub == 0
    seqs = min(RET_SAMPLE_SEQS, db)
    assert db % seqs == 0 and seq % PAGE == 0

    wi = w_in[0]
    q_a, kv_a, k_pe, rq, rk, rv, rg = (wi[:, 0:512], wi[:, 512:1024], wi[:, 1024:1088], wi[:, 1088:1600],
                                       wi[:, 1600:2112], wi[:, 2112:3136], wi[:, 3136:4160])
    w_in_r = jnp.concatenate([q_a, kv_a, rq, rk, rv, rg, k_pe, jnp.zeros((d, LANES - ROPE), F32)], axis=1)
    wqb = w_q_b[0]
    w = {
        "g_attn": g_attn[0][None, :], "w_in": w_in_r.astype(BF16),
        "g_q_a": g_q_a[0][None, :], "g_kv_a": g_kv_a[0][None, :],
        "w_qb": jnp.concatenate([wqb[:, :, :NOPE].reshape(KVL, HEADS * NOPE),
                                 jnp.pad(wqb[:, :, NOPE:], ((0, 0), (0, 0), (0, LANES - ROPE))).reshape(KVL, HEADS * LANES)],
                                axis=1).astype(BF16),
        "w_ukt": jnp.transpose(w_uk[0], (1, 2, 0)).astype(BF16),
        "w_o": w_o[0].astype(BF16), "g_ffn": g_ffn[0][None, :],
        "w_r": jnp.concatenate([w_router_group[0], w_router_expert[0],
                                jnp.zeros((d, LANES - N_GROUPS - N_GROUPS * EXPERTS_PER_GROUP), F32)],
                               axis=1).astype(BF16),
    }
    wuv = jnp.transpose(w_uv[0], (1, 0, 2)).astype(BF16)

    cs_p, sn_p = _rope_tables(jnp.arange(seq))
    cs_s, sn_s = _rope_tables(past_len + jnp.arange(t_new))
    cs_s, sn_s = jnp.tile(cs_s, (db, 1)), jnp.tile(sn_s, (db, 1))
    tiles_per_seq = seq // tm

    (c_p, kpe_p, q_p, rq_p, rk_p, rv_p, rg_p, kt, cb) = _inproj(
        x_prompt.reshape(tp, d), cs_p, sn_p, lambda i: i % tiles_per_seq, w, tm=tm, prompt=True)
    om_p = _attn_prompt(q_p, kt, cb, wuv, batch=batch, seq=seq, tq=TQ_ATTN, tk=tm)
    or_p, st_p = _retention(rq_p, rk_p, rv_p, rg_p, jnp.zeros((batch, HEADS, RET_DK, RET_DV), F32), g_ret[0],
                            seqs=1, t=PAGE, nc=seq // PAGE)

    (c_s, kpe_s, q_s, rq_s, rk_s, rv_s, rg_s) = _inproj(
        x_sample.reshape(ts, d), cs_s, sn_s, lambda i: i, w, tm=min(TM_PROJ_SAMPLE, ts), prompt=False)
    kpet_pool = jnp.swapaxes(cache_kpe.reshape(n_pool, PAGE, ROPE), 1, 2)
    o_lat_s = _attn_sample(page_table, q_s, c_s, kpe_s, cache_latent.reshape(n_pool, PAGE, KVL), kpet_pool,
                           t_new=t_new, pages=pages, sub=sub)
    om_s = _uv_sample(o_lat_s, wuv, t_new=t_new)
    or_s, st_s = _retention(rq_s, rk_s, rv_s, rg_s, state_ret[0], g_ret[0], seqs=seqs, t=t_new, nc=1)

    x2, hn, route = _outproj(x_prompt.reshape(tp, d), x_sample.reshape(ts, d), om_p, om_s, or_p, or_s, w, tm=tm)
    n_tok = tp + ts
    expert = jnp.concatenate([route[:, 2], route[:, 3]]).astype(jnp.int32)
    pos, src, tile_expert, tile_valid, tile_first = _sort_by_expert(expert, n_tok, TM_MOE)
    moe_sorted = _moe(src, tile_expert, tile_valid, tile_first, hn,
                      jnp.swapaxes(w_exp_gate, 2, 3), jnp.swapaxes(w_exp_up, 2, 3), w_exp_down, tmm=TM_MOE)
    gfin = g_final[None, :]
    tmf = min(TM_FINAL, ts)
    pos_p = jnp.concatenate([pos[:tp], pos[n_tok:n_tok + tp]])
    pos_s = jnp.concatenate([pos[tp:n_tok], pos[n_tok + tp:]])
    y_p = _final(pos_p, x2, route, moe_sorted, gfin, tm=tmf, tile0=0)
    y_s = _final(pos_s, x2, route, moe_sorted, gfin, tm=tmf, tile0=tp // tmf)

    return (y_p.reshape(batch, seq, d), y_s.reshape(db, t_new, d),
            c_p.reshape(1, batch, seq, KVL), kpe_p.reshape(1, batch, seq, ROPE), st_p[None],
            c_s.reshape(1, db, t_new, KVL), kpe_s.reshape(1, db, t_new, ROPE), st_s[None])
```

```python
import functools

import jax
import jax.numpy as jnp
from jax import lax
from jax.experimental import pallas as pl
from jax.experimental.pallas import tpu as pltpu

F32 = jnp.float32
BF16 = jnp.bfloat16

HEADS = 8
NOPE = 128
ROPE = 64
KVL = 512
VDIM = 128
RET_DK = 64
RET_DV = 128
PAGE = 128
N_GROUPS = 4
EXPERTS_PER_GROUP = 4
EPS = 1e-6
ROPE_BASE = 10000.0
NEG = -1e30
LOG2E = 1.4426950408889634
Q_SCALE = (NOPE + ROPE) ** -0.5 * LOG2E
QK_W = KVL + 128

V7X_VMEM_BYTES = 64 * 1024 * 1024
VMEM_LIMIT = V7X_VMEM_BYTES - 4 * 1024 * 1024
LANES = 128

TM_PROJ = 512
TM_PROJ_SAMPLE = 256
TQ_ATTN = 256
PAGES_PER_STEP = 32
PAGES_PER_CHAIN = 32
RET_SAMPLE_SEQS = 16
TM_MOE = 256
TM_FINAL = 256


def _mm(a, b):
    return jnp.dot(a, b, preferred_element_type=F32)


def _mm_nt(a, b):
    return lax.dot_general(a, b, (((1,), (1,)), ((), ())), preferred_element_type=F32)


def _mm_tn(a, b):
    return lax.dot_general(a, b, (((0,), (0,)), ((), ())), preferred_element_type=F32)


def _rms(x, g):
    return x * lax.rsqrt(jnp.mean(x * x, axis=-1, keepdims=True) + EPS) * g


def _silu(x):
    return x * (1.0 / (1.0 + jnp.exp(-x)))


def _rope(x, cos, sin):
    w = x.shape[-1]
    lane = lax.broadcasted_iota(jnp.int32, x.shape, 1)
    first_half = (lane % ROPE) < (ROPE // 2)
    swapped = jnp.where(first_half, pltpu.roll(x, w - ROPE // 2, 1), pltpu.roll(x, ROPE // 2, 1))
    return x * cos + swapped * sin


def _const_spec(shape):
    nd = len(shape)
    return pl.BlockSpec(shape, lambda *_: (0,) * nd, pipeline_mode=pl.Buffered(1))


def _params(*sem):
    return pltpu.CompilerParams(dimension_semantics=sem, vmem_limit_bytes=VMEM_LIMIT)


def _inproj_kernel(x_ref, cs_ref, sn_ref, gattn_ref, win_ref, gq_ref, gkv_ref, wqb_ref, wukt_ref,
                   c_ref, kpe_ref, q_ref, rq_ref, rk_ref, rv_ref, rg_ref, *rest, prompt):
    h = _rms(x_ref[...], gattn_ref[...]).astype(BF16)
    cs = cs_ref[...]
    sn = sn_ref[...]
    cs4 = jnp.concatenate([cs] * 4, axis=1)
    sn4 = jnp.concatenate([sn] * 4, axis=1)
    cs8 = jnp.concatenate([cs4] * 2, axis=1)
    sn8 = jnp.concatenate([sn4] * 2, axis=1)

    def proj(lo, hi):
        return _mm(h, win_ref[:, lo:hi])

    c = _rms(proj(512, 1024), gkv_ref[...])
    c_ref[...] = c
    kpe128 = _rope(proj(4096, 4224), cs, sn)
    kpe_ref[...] = kpe128[:, :ROPE]

    qn = _rms(proj(0, 512), gq_ref[...]).astype(BF16)
    q = _mm(qn, wqb_ref[...])
    q_pe = _rope(q[:, HEADS * NOPE:], cs8, sn8) * Q_SCALE
    for hh in range(HEADS):
        q_nope = q[:, hh * NOPE:(hh + 1) * NOPE].astype(BF16)
        q_lat = _mm(q_nope, wukt_ref[hh]) * Q_SCALE
        q_ref[hh] = jnp.concatenate([q_lat, q_pe[:, hh * LANES:(hh + 1) * LANES]], axis=1).astype(q_ref.dtype)

    rq_ref[...] = (_rope(proj(1024, 1536), cs4, sn4) * (RET_DK ** -0.5)).astype(rq_ref.dtype)
    rk_ref[...] = _rope(proj(1536, 2048), cs4, sn4).astype(rk_ref.dtype)
    rv_ref[...] = proj(2048, 3072).astype(rv_ref.dtype)
    rg_ref[...] = _silu(proj(3072, 4096)).astype(rg_ref.dtype)

    if prompt:
        kt_ref, cb_ref = rest
        cb_ref[...] = c.astype(BF16)
        kt_ref[0] = jnp.concatenate([c.T, kpe128.T], axis=0).astype(BF16)


def _inproj(x2d, cs, sn, cs_index, w, *, tm, prompt):
    t_tok, d = x2d.shape
    nt = t_tok // tm
    adt = BF16 if prompt else F32
    row = lambda width: pl.BlockSpec((tm, width), lambda i: (i, 0))
    out_shape = [
        jax.ShapeDtypeStruct((t_tok, KVL), F32), jax.ShapeDtypeStruct((t_tok, ROPE), F32),
        jax.ShapeDtypeStruct((HEADS, t_tok, QK_W), adt),
        jax.ShapeDtypeStruct((t_tok, HEADS * RET_DK), adt), jax.ShapeDtypeStruct((t_tok, HEADS * RET_DK), adt),
        jax.ShapeDtypeStruct((t_tok, HEADS * RET_DV), adt), jax.ShapeDtypeStruct((t_tok, HEADS * RET_DV), adt),
    ]
    out_specs = [
        row(KVL), row(ROPE),
        pl.BlockSpec((HEADS, tm, QK_W), lambda i: (0, i, 0)),
        row(HEADS * RET_DK), row(HEADS * RET_DK), row(HEADS * RET_DV), row(HEADS * RET_DV),
    ]
    if prompt:
        out_shape += [jax.ShapeDtypeStruct((nt, QK_W, tm), BF16), jax.ShapeDtypeStruct((t_tok, KVL), BF16)]
        out_specs += [pl.BlockSpec((1, QK_W, tm), lambda i: (i, 0, 0)), row(KVL)]
    in_specs = [
        row(d),
        pl.BlockSpec((tm, LANES), lambda i: (cs_index(i), 0)), pl.BlockSpec((tm, LANES), lambda i: (cs_index(i), 0)),
        _const_spec(w["g_attn"].shape), _const_spec(w["w_in"].shape), _const_spec(w["g_q_a"].shape),
        _const_spec(w["g_kv_a"].shape), _const_spec(w["w_qb"].shape), _const_spec(w["w_ukt"].shape),
    ]
    return pl.pallas_call(
        functools.partial(_inproj_kernel, prompt=prompt),
        grid=(nt,), in_specs=in_specs, out_specs=out_specs, out_shape=out_shape,
        compiler_params=_params("parallel"), name="inproj_prompt" if prompt else "inproj_sample",
    )(x2d, cs, sn, w["g_attn"], w["w_in"], w["g_q_a"], w["g_kv_a"], w["w_qb"], w["w_ukt"])


def _attn_prompt_kernel(q_ref, kt_ref, cb_ref, wuv_ref, o_ref, m_ref, l_ref, acc_ref, sa_ref, sb_ref, *, tq, tk):
    qi = pl.program_id(1)
    rows = HEADS * tq
    q = q_ref[...].reshape(rows, QK_W)
    m_ref[...] = jnp.full((rows, 1), NEG, F32)
    l_ref[...] = jnp.zeros((rows, 1), F32)
    acc_ref[...] = jnp.zeros((rows, KVL), F32)

    def scores(j, s_ref):
        s_ref[...] = _mm(q, kt_ref[j])

    def consume(s_ref, j, masked):
        s = s_ref[...]
        if masked:
            q_pos = lax.broadcasted_iota(jnp.int32, (rows, tk), 0) % tq + qi * tq
            k_pos = lax.broadcasted_iota(jnp.int32, (rows, tk), 1) + j * tk
            s = jnp.where(k_pos <= q_pos, s, NEG)
        m_prev = m_ref[...]
        m_new = jnp.maximum(m_prev, jnp.max(s, axis=1, keepdims=True))
        alpha = jnp.exp2(m_prev - m_new)
        p = jnp.exp2(s - m_new)
        l_ref[...] = alpha * l_ref[...] + jnp.sum(p, axis=1, keepdims=True)
        v = cb_ref[pl.ds(pl.multiple_of(j * tk, tk), tk), :]
        acc_ref[...] = alpha * acc_ref[...] + _mm(p.astype(BF16), v)
        m_ref[...] = m_new

    last = (qi * tq) // tk
    scores(0, sa_ref)

    def pair(i, carry):
        j = 2 * i
        scores(j + 1, sb_ref)
        consume(sa_ref, j, False)
        scores(j + 2, sa_ref)
        consume(sb_ref, j + 1, False)
        return carry

    lax.fori_loop(0, last // 2, pair, 0)

    @pl.when(last % 2 == 0)
    def _():
        consume(sa_ref, last, True)

    @pl.when(last % 2 == 1)
    def _():
        scores(last, sb_ref)
        consume(sa_ref, last - 1, False)
        consume(sb_ref, last, True)

    o = acc_ref[...] / l_ref[...]
    for hh in range(HEADS):
        oh = o[hh * tq:(hh + 1) * tq].astype(BF16)
        o_ref[:, hh * VDIM:(hh + 1) * VDIM] = _mm(oh, wuv_ref[hh]).astype(o_ref.dtype)


def _attn_prompt(q, kt, cb, wuv, *, batch, seq, tq, tk):
    nq = seq // tq
    nkb = seq // tk
    rows = HEADS * tq
    return pl.pallas_call(
        functools.partial(_attn_prompt_kernel, tq=tq, tk=tk),
        grid=(batch, nq),
        in_specs=[
            pl.BlockSpec((HEADS, tq, QK_W), lambda b, i: (0, b * nq + i, 0)),
            pl.BlockSpec((nkb, QK_W, tk), lambda b, i: (b, 0, 0)),
            pl.BlockSpec((seq, KVL), lambda b, i: (b, 0)),
            _const_spec(wuv.shape),
        ],
        out_specs=pl.BlockSpec((tq, HEADS * VDIM), lambda b, i: (b * nq + i, 0)),
        out_shape=jax.ShapeDtypeStruct((batch * seq, HEADS * VDIM), BF16),
        scratch_shapes=[pltpu.VMEM((rows, 1), F32), pltpu.VMEM((rows, 1), F32), pltpu.VMEM((rows, KVL), F32),
                        pltpu.VMEM((rows, tk), F32), pltpu.VMEM((rows, tk), F32)],
        compiler_params=_params("parallel", "arbitrary"), name="attn_prompt",
    )(q, kt, cb, wuv)


def _attn_sample_kernel(pt_ref, q_ref, cnew_ref, knew_ref, lat_hbm, kpet_hbm, o_ref,
                        cbuf0, cbuf1, cbuf2, kbuf0, kbuf1, kbuf2, sem, qt_ref, qpe_ref, m_ref, l_ref, acc_ref,
                        *, pages, sub, nblk, t_new):
    cbufs = (cbuf0, cbuf1, cbuf2)
    kbufs = (kbuf0, kbuf1, kbuf2)
    b = pl.program_id(0)
    j = pl.program_id(1)
    step = b * nblk + j
    total = pl.num_programs(0) * nblk
    slot = step % 3
    rows = HEADS * t_new

    def start_step_pages(s, sl):
        bb = s // nblk
        jj = s % nblk
        for p in range(pages):
            page = pt_ref[bb, jj * pages + p]
            pltpu.make_async_copy(lat_hbm.at[page], cbufs[sl].at[p], sem.at[0, sl]).start()
            pltpu.make_async_copy(kpet_hbm.at[page], kbufs[sl].at[p], sem.at[1, sl]).start()

    def wait_pages(sl):
        pltpu.make_async_copy(lat_hbm.at[pl.ds(0, pages)], cbufs[sl], sem.at[0, sl]).wait()
        pltpu.make_async_copy(kpet_hbm.at[pl.ds(0, pages)], kbufs[sl], sem.at[1, sl]).wait()

    @pl.when(step == 0)
    def _():
        start_step_pages(0, 0)
        start_step_pages(jnp.minimum(1, total - 1), 1)

    @pl.when(j == 0)
    def _():
        q = q_ref[...].reshape(rows, QK_W)
        zpad = jnp.zeros((LANES - rows, QK_W), F32)
        qt_ref[...] = jnp.concatenate([q, zpad], axis=0).T.astype(BF16)
        qpe_ref[...] = q[:, KVL:KVL + ROPE].astype(BF16)
        m_ref[...] = jnp.full((rows, 1), NEG, F32)
        l_ref[...] = jnp.zeros((rows, 1), F32)
        acc_ref[...] = jnp.zeros((rows, KVL), F32)

    def partial_softmax(s, v):
        m_c = jnp.max(s, axis=1, keepdims=True)
        p = jnp.exp2(s - m_c)
        return m_c, jnp.sum(p, axis=1, keepdims=True), _mm(p.astype(BF16), v)

    def merge(parts):
        m_prev = m_ref[...]
        m_new = m_prev
        for m_c, _, _ in parts:
            m_new = jnp.maximum(m_new, m_c)
        alpha = jnp.exp2(m_prev - m_new)
        l_new = alpha * l_ref[...]
        acc = alpha * acc_ref[...]
        for m_c, l_c, a_c in parts:
            w_c = jnp.exp2(m_c - m_new)
            l_new = l_new + w_c * l_c
            acc = acc + w_c * a_c
        m_ref[...] = m_new
        l_ref[...] = l_new
        acc_ref[...] = acc

    nxt = jnp.minimum(step + 2, total - 1)
    for sl in range(3):
        @pl.when(slot == sl)
        def _(sl=sl):
            wait_pages(sl)
            start_step_pages(nxt, (sl + 2) % 3)
            q_lat_t = qt_ref[0:KVL, :]
            q_pe = qpe_ref[...]
            parts = []
            for c0 in range(0, pages, sub):
                kc = cbufs[sl][c0:c0 + sub].reshape(sub * PAGE, KVL).astype(BF16)
                kpt = jnp.concatenate([kbufs[sl][c0 + p] for p in range(sub)], axis=1).astype(BF16)
                parts.append(partial_softmax(_mm(kc, q_lat_t).T[:rows] + _mm(q_pe, kpt), kc))
            merge(parts)

    @pl.when(j == nblk - 1)
    def _():
        q_lat_t = qt_ref[0:KVL, :]
        q_pe = qpe_ref[...]
        zc = jnp.zeros((LANES - t_new, KVL), F32)
        zk = jnp.zeros((LANES - t_new, ROPE), F32)
        cn = jnp.concatenate([cnew_ref[...], zc], axis=0).astype(BF16)
        kn = jnp.concatenate([knew_ref[...], zk], axis=0).astype(BF16)
        s2 = _mm(cn, q_lat_t).T[:rows] + _mm_nt(q_pe, kn)
        q_t = lax.broadcasted_iota(jnp.int32, (rows, LANES), 0) % t_new
        k_t = lax.broadcasted_iota(jnp.int32, (rows, LANES), 1)
        merge([partial_softmax(jnp.where(k_t <= q_t, s2, NEG), cn)])
        o_ref[0] = acc_ref[...] / l_ref[...]

    for sl in range(3):
        @pl.when((step == total - 1) & (slot == sl))
        def _(sl=sl):
            wait_pages((sl + 1) % 3)
            wait_pages((sl + 2) % 3)


def _attn_sample(page_table, q, c_new, kpe_new, lat_pool, kpet_pool, *, t_new, pages, sub):
    db, n_pages = page_table.shape
    nblk = n_pages // pages
    rows = HEADS * t_new
    grid_spec = pltpu.PrefetchScalarGridSpec(
        num_scalar_prefetch=1, grid=(db, nblk),
        in_specs=[
            pl.BlockSpec((HEADS, t_new, QK_W), lambda b, j, pt: (0, b, 0)),
            pl.BlockSpec((t_new, KVL), lambda b, j, pt: (b, 0)),
            pl.BlockSpec((t_new, ROPE), lambda b, j, pt: (b, 0)),
            pl.BlockSpec(memory_space=pl.ANY), pl.BlockSpec(memory_space=pl.ANY),
        ],
        out_specs=pl.BlockSpec((1, rows, KVL), lambda b, j, pt: (b, 0, 0)),
        scratch_shapes=[
            pltpu.VMEM((pages, PAGE, KVL), F32), pltpu.VMEM((pages, PAGE, KVL), F32),
            pltpu.VMEM((pages, PAGE, KVL), F32),
            pltpu.VMEM((pages, ROPE, PAGE), F32), pltpu.VMEM((pages, ROPE, PAGE), F32),
            pltpu.VMEM((pages, ROPE, PAGE), F32),
            pltpu.SemaphoreType.DMA((2, 3)),
            pltpu.VMEM((QK_W, LANES), BF16), pltpu.VMEM((rows, ROPE), BF16),
            pltpu.VMEM((rows, 1), F32), pltpu.VMEM((rows, 1), F32), pltpu.VMEM((rows, KVL), F32),
        ],
    )
    return pl.pallas_call(
        functools.partial(_attn_sample_kernel, pages=pages, sub=sub, nblk=nblk, t_new=t_new),
        grid_spec=grid_spec, out_shape=jax.ShapeDtypeStruct((db, rows, KVL), F32),
        compiler_params=_params("arbitrary", "arbitrary"), name="attn_sample",
    )(page_table, q, c_new, kpe_new, lat_pool, kpet_pool)


def _uv_sample_kernel(o_ref, wuv_ref, out_ref):
    n, t_new, _ = o_ref.shape
    out_ref[...] = _mm(o_ref[...].reshape(n * t_new, KVL).astype(BF16), wuv_ref[0]).astype(out_ref.dtype)


def _uv_sample(o_lat, wuv, *, t_new):
    db = o_lat.shape[0]
    o4 = o_lat.reshape(db, HEADS, t_new, KVL)
    return pl.pallas_call(
        _uv_sample_kernel, grid=(HEADS,),
        in_specs=[pl.BlockSpec((db, None, t_new, KVL), lambda h: (0, h, 0, 0)),
                  pl.BlockSpec((1, KVL, VDIM), lambda h: (h, 0, 0))],
        out_specs=pl.BlockSpec((db * t_new, VDIM), lambda h: (0, h)),
        out_shape=jax.ShapeDtypeStruct((db * t_new, HEADS * VDIM), BF16),
        compiler_params=_params("parallel"), name="uv_sample",
    )(o4, wuv)


def _ret_kernel(rq_ref, rk_ref, rv_ref, rg_ref, s0_ref, di_ref, dq_ref, dk_ref, ds_ref, gr_ref,
                o_ref, sf_ref, st_ref, *, seqs, t, nc):
    c = pl.program_id(1)

    @pl.when(c == 0)
    def _():
        st_ref[...] = s0_ref[...]

    q = rq_ref[...]
    k = rk_ref[...]
    v = rv_ref[...]
    outs = []
    for hh in range(HEADS):
        qh = q[:, hh * RET_DK:(hh + 1) * RET_DK]
        kh = k[:, hh * RET_DK:(hh + 1) * RET_DK]
        vh = v[:, hh * RET_DV:(hh + 1) * RET_DV]
        a = _mm_nt(qh.astype(BF16), kh.astype(BF16)) * di_ref[hh]
        o = _mm(a.astype(BF16), vh.astype(BF16))
        kd = kh.astype(F32) * dk_ref[hh]
        cross = []
        for g in range(seqs):
            sl = slice(g * t, (g + 1) * t)
            state = st_ref[g, hh]
            cross.append(_mm(qh[sl].astype(BF16), state.astype(BF16)))
            st_ref[g, hh] = state * ds_ref[hh] + _mm_tn(kd[sl].astype(BF16), vh[sl].astype(BF16))
        o = o + (cross[0] if seqs == 1 else jnp.concatenate(cross, axis=0)) * dq_ref[hh]
        outs.append(_rms(o, gr_ref[hh:hh + 1, :]))
    o_ref[...] = (rg_ref[...].astype(F32) * jnp.concatenate(outs, axis=1)).astype(o_ref.dtype)

    @pl.when(c == nc - 1)
    def _():
        sf_ref[...] = st_ref[...]


def _retention(rq, rk, rv, rg, s0, g_ret, *, seqs, t, nc):
    rows = seqs * t
    n_state = s0.shape[0]
    nb = n_state // seqs
    hh = jnp.arange(HEADS, dtype=F32)
    log_g = jnp.log1p(-jnp.exp2(-5.0 - hh))
    i = jnp.arange(t, dtype=F32)
    diff = i[:, None] - i[None, :]
    d_intra = jnp.where(diff >= 0, jnp.exp(jnp.maximum(diff, 0.0)[None] * log_g[:, None, None]), 0.0)
    d_q = jnp.exp((i[None, :] + 1.0) * log_g[:, None])
    d_k = jnp.exp((t - 1.0 - i)[None, :] * log_g[:, None])
    d_s = jnp.exp(t * log_g)
    eye = jnp.eye(seqs, dtype=F32)
    di = jnp.einsum("gk,hij->hgikj", eye, d_intra).reshape(HEADS, rows, rows)
    dq = jnp.broadcast_to(jnp.tile(d_q, (1, seqs))[:, :, None], (HEADS, rows, RET_DV))
    dk = jnp.broadcast_to(jnp.tile(d_k, (1, seqs))[:, :, None], (HEADS, rows, RET_DK))
    ds = jnp.broadcast_to(d_s[:, None, None], (HEADS, RET_DK, RET_DV))
    blk = lambda width: pl.BlockSpec((rows, width), lambda s, c: (s * nc + c, 0))
    state_spec = pl.BlockSpec((seqs, HEADS, RET_DK, RET_DV), lambda s, c: (s, 0, 0, 0))
    return pl.pallas_call(
        functools.partial(_ret_kernel, seqs=seqs, t=t, nc=nc),
        grid=(nb, nc),
        in_specs=[blk(HEADS * RET_DK), blk(HEADS * RET_DK), blk(HEADS * RET_DV), blk(HEADS * RET_DV), state_spec,
                  _const_spec(di.shape), _const_spec(dq.shape), _const_spec(dk.shape), _const_spec(ds.shape),
                  _const_spec(g_ret.shape)],
        out_specs=[blk(HEADS * RET_DV), state_spec],
        out_shape=[jax.ShapeDtypeStruct((nb * nc * rows, HEADS * RET_DV), BF16),
                   jax.ShapeDtypeStruct(s0.shape, F32)],
        scratch_shapes=[pltpu.VMEM((seqs, HEADS, RET_DK, RET_DV), F32)],
        compiler_params=_params("parallel", "arbitrary"), name="retention_t%d" % t,
    )(rq, rk, rv, rg, s0, di, dq, dk, ds, g_ret)


def _outproj_kernel(xp_ref, xs_ref, omp_ref, oms_ref, orp_ref, ors_ref, wo_ref, gffn_ref, wr_ref,
                    x2_ref, hn_ref, route_ref, *, n_prompt_tiles):
    from_prompt = pl.program_id(0) < n_prompt_tiles
    x = jnp.where(from_prompt, xp_ref[...], xs_ref[...])
    o_mla = jnp.where(from_prompt, omp_ref[...], oms_ref[...])
    o_ret = jnp.where(from_prompt, orp_ref[...], ors_ref[...])
    half = HEADS * VDIM
    x2 = x + _mm(o_mla, wo_ref[0:half, :]) + _mm(o_ret, wo_ref[half:, :])
    x2_ref[...] = x2
    hn = _rms(x2, gffn_ref[...])
    hn_ref[...] = hn
    logits = _mm(hn.astype(BF16), wr_ref[...])
    logits_t = logits.T

    def col(kk):
        return logits_t[kk:kk + 1, :]

    best = col(0)
    grp = jnp.zeros(best.shape, jnp.int32)
    for kk in range(1, N_GROUPS):
        upd = col(kk) > best
        grp = jnp.where(upd, kk, grp)
        best = jnp.where(upd, col(kk), best)
    den = jnp.exp(col(0) - best)
    for kk in range(1, N_GROUPS):
        den = den + jnp.exp(col(kk) - best)
    gate = 1.0 / den

    le = []
    for jj in range(EXPERTS_PER_GROUP):
        sel = col(N_GROUPS + (N_GROUPS - 1) * EXPERTS_PER_GROUP + jj)
        for gg in range(N_GROUPS - 2, -1, -1):
            sel = jnp.where(grp == gg, col(N_GROUPS + gg * EXPERTS_PER_GROUP + jj), sel)
        le.append(sel)
    mx = jnp.maximum(jnp.maximum(le[0], le[1]), jnp.maximum(le[2], le[3]))
    ex = [jnp.exp(v - mx) for v in le]
    chosen = []
    for jj in range(EXPERTS_PER_GROUP):
        rank = jnp.zeros(best.shape, jnp.int32)
        for ii in range(EXPERTS_PER_GROUP):
            if ii == jj:
                continue
            ahead = (ex[ii] > ex[jj]) | ((ex[ii] == ex[jj]) & (ii < jj))
            rank = rank + ahead.astype(jnp.int32)
        chosen.append(rank < 2)
    den2 = jnp.zeros(best.shape, F32)
    for jj in range(EXPERTS_PER_GROUP):
        den2 = den2 + jnp.where(chosen[jj], ex[jj], 0.0)
    lo = jnp.full(best.shape, EXPERTS_PER_GROUP, jnp.int32)
    hi = jnp.full(best.shape, -1, jnp.int32)
    for jj in range(EXPERTS_PER_GROUP):
        lo = jnp.where(chosen[jj], jnp.minimum(lo, jj), lo)
        hi = jnp.where(chosen[jj], jnp.maximum(hi, jj), hi)
    w_lo = jnp.zeros(best.shape, F32)
    w_hi = jnp.zeros(best.shape, F32)
    for jj in range(EXPERTS_PER_GROUP):
        comb = ex[jj] / den2 * gate
        w_lo = jnp.where(lo == jj, comb, w_lo)
        w_hi = jnp.where(hi == jj, comb, w_hi)
    tm = logits.shape[0]
    sub = lax.broadcasted_iota(jnp.int32, (8, tm), 0)
    base = grp * EXPERTS_PER_GROUP
    route_t = jnp.where(sub == 0, w_lo, 0.0)
    route_t = jnp.where(sub == 1, w_hi, route_t)
    route_t = jnp.where(sub == 2, (base + lo).astype(F32), route_t)
    route_t = jnp.where(sub == 3, (base + hi).astype(F32), route_t)
    route_ref[...] = jnp.concatenate([route_t, jnp.zeros((LANES - 8, tm), F32)], axis=0).T


def _outproj(x_p, x_s, om_p, om_s, or_p, or_s, w, *, tm):
    tp, d = x_p.shape
    ts = x_s.shape[0]
    npt = tp // tm
    nst = ts // tm
    p_idx = lambda i: (jnp.minimum(i, npt - 1), 0)
    s_idx = lambda i: (jnp.maximum(i - npt, 0), 0)
    mix = om_p.shape[1]
    return pl.pallas_call(
        functools.partial(_outproj_kernel, n_prompt_tiles=npt),
        grid=(npt + nst,),
        in_specs=[pl.BlockSpec((tm, d), p_idx), pl.BlockSpec((tm, d), s_idx),
                  pl.BlockSpec((tm, mix), p_idx), pl.BlockSpec((tm, mix), s_idx),
                  pl.BlockSpec((tm, mix), p_idx), pl.BlockSpec((tm, mix), s_idx),
                  _const_spec(w["w_o"].shape), _const_spec(w["g_ffn"].shape), _const_spec(w["w_r"].shape)],
        out_specs=[pl.BlockSpec((tm, d), lambda i: (i, 0)), pl.BlockSpec((tm, d), lambda i: (i, 0)),
                   pl.BlockSpec((tm, LANES), lambda i: (i, 0))],
        out_shape=[jax.ShapeDtypeStruct((tp + ts, d), F32), jax.ShapeDtypeStruct((tp + ts, d), F32),
                   jax.ShapeDtypeStruct((tp + ts, LANES), F32)],
        compiler_params=_params("parallel"), name="outproj_router",
    )(x_p, x_s, om_p, om_s, or_p, or_s, w["w_o"], w["g_ffn"], w["w_r"])


def _start_row_gather(idx_ref, base, n, src_hbm, buf, sem):
    for r in range(n):
        pltpu.make_async_copy(src_hbm.at[pl.ds(idx_ref[base + r], 1)], buf.at[pl.ds(r, 1)], sem).start()


def _wait_row_gather(n, src_hbm, buf, sem):
    pltpu.make_async_copy(src_hbm.at[pl.ds(0, n)], buf, sem).wait()


def _moe_kernel(src_ref, te_ref, tv_ref, tf_ref, nv_ref, hn_hbm, wgt_ref, wut_ref, wd_ref, out_ref,
                buf0, buf1, buf2, sem, wb_ref, *, tmm):
    bufs = (buf0, buf1, buf2)
    i = pl.program_id(0)
    nv = nv_ref[0]
    slot = i % 3
    nxt = jnp.minimum(i + 2, nv - 1)

    def start(base, buf, s):
        _start_row_gather(src_ref, base, tmm, hn_hbm, buf, s)

    def wait(buf, s):
        _wait_row_gather(tmm, hn_hbm, buf, s)

    @pl.when(i == 0)
    def _():
        start(0, bufs[0], sem.at[0])
        start(jnp.minimum(1, nv - 1) * tmm, bufs[1], sem.at[1])

    @pl.when(tf_ref[i] == 1)
    def _():
        wb_ref[0] = wgt_ref[0].astype(BF16)
        wb_ref[1] = wut_ref[0].astype(BF16)
        wb_ref[2] = wd_ref[0].astype(BF16)

    for sl in range(3):
        ah = (sl + 2) % 3

        @pl.when((tv_ref[i] == 1) & (slot == sl))
        def _(sl=sl, ah=ah):
            wait(bufs[sl], sem.at[sl])
            start(nxt * tmm, bufs[ah], sem.at[ah])
            hb = bufs[sl][...].astype(BF16)
            a = _silu(_mm_nt(hb, wb_ref[0])) * _mm_nt(hb, wb_ref[1])
            out_ref[...] = _mm(a.astype(BF16), wb_ref[2])

        @pl.when((i == nv - 1) & (slot == sl))
        def _(sl=sl, ah=ah):
            wait(bufs[(sl + 1) % 3], sem.at[(sl + 1) % 3])
            wait(bufs[ah], sem.at[ah])

    @pl.when(tv_ref[i] == 0)
    def _():
        out_ref[...] = jnp.zeros(out_ref.shape, F32)


def _moe(src, tile_expert, tile_valid, tile_first, hn, wgt, wut, wd, *, tmm):
    n_rows = src.shape[0]
    nt = n_rows // tmm
    d = hn.shape[1]
    f = wd.shape[2]
    wspec = pl.BlockSpec((None, 1, f, d), lambda i, s, te, tv, tf, nv: (0, te[i], 0, 0))
    row_buf = pltpu.VMEM((tmm, d), F32)
    n_valid = jnp.sum(tile_valid, keepdims=True).astype(jnp.int32)
    grid_spec = pltpu.PrefetchScalarGridSpec(
        num_scalar_prefetch=5, grid=(nt,),
        in_specs=[pl.BlockSpec(memory_space=pl.ANY), wspec, wspec, wspec],
        out_specs=pl.BlockSpec((tmm, d), lambda i, s, te, tv, tf, nv: (i, 0)),
        scratch_shapes=[row_buf, row_buf, row_buf,
                        pltpu.SemaphoreType.DMA((3,)), pltpu.VMEM((3, f, d), BF16)],
    )
    return pl.pallas_call(
        functools.partial(_moe_kernel, tmm=tmm),
        grid_spec=grid_spec, out_shape=jax.ShapeDtypeStruct((n_rows, d), F32),
        compiler_params=_params("arbitrary"), name="moe_expert_sorted",
    )(src, tile_expert, tile_valid, tile_first, n_valid, hn, wgt, wut, wd)


def _final_kernel(pos_ref, x2_ref, route_ref, moe_hbm, gfin_ref, y_ref, buf, sem, *, tm, n_tok):
    i = pl.program_id(0)
    nt = pl.num_programs(0)
    slot = i % 2
    nxt = jnp.minimum(i + 1, nt - 1)

    def start(tile, sl):
        _start_row_gather(pos_ref, tile * tm, tm, moe_hbm, buf.at[sl, 0], sem.at[sl])
        _start_row_gather(pos_ref, n_tok + tile * tm, tm, moe_hbm, buf.at[sl, 1], sem.at[sl])

    def wait(sl):
        _wait_row_gather(tm, moe_hbm, buf.at[sl, 0], sem.at[sl])
        _wait_row_gather(tm, moe_hbm, buf.at[sl, 1], sem.at[sl])

    @pl.when(i == 0)
    def _():
        start(0, 0)

    start(nxt, 1 - slot)
    wait(slot)
    route = route_ref[...]
    moe = route[:, 0:1] * buf[slot, 0] + route[:, 1:2] * buf[slot, 1]
    y_ref[...] = _rms(x2_ref[...] + moe, gfin_ref[...])

    @pl.when(i == nt - 1)
    def _():
        wait(1 - slot)


def _final(pos, x2, route, moe_sorted, g_final, *, tm, tile0):
    n = pos.shape[0] // 2
    d = x2.shape[1]
    grid_spec = pltpu.PrefetchScalarGridSpec(
        num_scalar_prefetch=1, grid=(n // tm,),
        in_specs=[pl.BlockSpec((tm, d), lambda i, p: (i + tile0, 0)),
                  pl.BlockSpec((tm, LANES), lambda i, p: (i + tile0, 0)),
                  pl.BlockSpec(memory_space=pl.ANY),
                  pl.BlockSpec(g_final.shape, lambda i, p: (0, 0))],
        out_specs=pl.BlockSpec((tm, d), lambda i, p: (i, 0)),
        scratch_shapes=[pltpu.VMEM((2, 2, tm, d), F32), pltpu.SemaphoreType.DMA((2,))],
    )
    return pl.pallas_call(
        functools.partial(_final_kernel, tm=tm, n_tok=n),
        grid_spec=grid_spec, out_shape=jax.ShapeDtypeStruct((n, d), F32),
        compiler_params=_params("arbitrary"), name="final_norm",
    )(pos, x2, route, moe_sorted, g_final)


def _rope_tables(pos):
    inv = 1.0 / (ROPE_BASE ** (jnp.arange(0, ROPE, 2, dtype=F32) / ROPE))
    ang = pos.astype(F32)[:, None] * inv[None, :]
    cos = jnp.cos(ang)
    sin = jnp.sin(ang)
    return jnp.concatenate([cos] * 4, axis=1), jnp.concatenate([-sin, sin] * 2, axis=1)


def _sort_by_expert(expert, n_tok, tmm):
    n_exp = N_GROUPS * EXPERTS_PER_GROUP
    n = expert.shape[0]
    onehot = (expert[:, None] == jnp.arange(n_exp, dtype=jnp.int32)[None, :]).astype(jnp.int32)
    csum = jnp.cumsum(onehot, axis=0)
    counts = csum[-1]
    rank = jnp.take_along_axis(csum, expert[:, None], axis=1)[:, 0] - 1
    padded = ((counts + tmm - 1) // tmm) * tmm
    ends = jnp.cumsum(padded)
    pos = ((ends - padded)[expert] + rank).astype(jnp.int32)
    n_tiles = n // tmm + n_exp
    token = jnp.arange(n, dtype=jnp.int32) % n_tok
    src = (jnp.arange(n_tiles * tmm, dtype=jnp.int32) % n_tok).at[pos].set(token)
    starts = jnp.arange(n_tiles, dtype=jnp.int32) * tmm
    te = jnp.sum((starts[:, None] >= ends[None, :]).astype(jnp.int32), axis=1)
    valid = (te < n_exp).astype(jnp.int32)
    te = jnp.minimum(te, n_exp - 1).astype(jnp.int32)
    first = valid * jnp.concatenate([jnp.ones((1,), jnp.int32), (te[1:] != te[:-1]).astype(jnp.int32)])
    return pos, src, te, valid, first


def kernel(x_prompt, x_sample, cache_latent, cache_kpe, state_ret, page_table, g_attn, w_in, g_q_a, w_q_b,
           g_kv_a, w_uk, w_uv, g_ret, w_o, g_ffn, w_router_group, w_router_expert, w_exp_gate, w_exp_up,
           w_exp_down, g_final):
    batch, seq, d = x_prompt.shape
    db, t_new, _ = x_sample.shape
    depth = w_in.shape[0]
    assert depth == 1, "single-layer step"
    n_pool = cache_latent.shape[1]
    past_len = page_table.shape[1] * PAGE
    tp, ts = batch * seq, db * t_new
    tm = min(TM_PROJ, seq, ts)
    assert seq % tm == 0 and ts % tm == 0 and tm % TQ_ATTN == 0 and (tp + ts) % TM_MOE == 0
    pages = min(PAGES_PER_STEP, page_table.shape[1])
    sub = min(PAGES_PER_CHAIN, pages)
    assert page_table.shape[1] % pages == 0 and pages % sub == 0
    seqs = min(RET_SAMPLE_SEQS, db)
    assert db % seqs == 0 and seq % PAGE == 0

    wi = w_in[0]
    q_a, kv_a, k_pe, rq, rk, rv, rg = (wi[:, 0:512], wi[:, 512:1024], wi[:, 1024:1088], wi[:, 1088:1600],
                                       wi[:, 1600:2112], wi[:, 2112:3136], wi[:, 3136:4160])
    w_in_r = jnp.concatenate([q_a, kv_a, rq, rk, rv, rg, k_pe, jnp.zeros((d, LANES - ROPE), F32)], axis=1)
    wqb = w_q_b[0]
    w = {
        "g_attn": g_attn[0][None, :], "w_in": w_in_r.astype(BF16),
        "g_q_a": g_q_a[0][None, :], "g_kv_a": g_kv_a[0][None, :],
        "w_qb": jnp.concatenate([wqb[:, :, :NOPE].reshape(KVL, HEADS * NOPE),
                                 jnp.pad(wqb[:, :, NOPE:], ((0, 0), (0, 0), (0, LANES - ROPE))).reshape(KVL, HEADS * LANES)],
                                axis=1).astype(BF16),
        "w_ukt": jnp.transpose(w_uk[0], (1, 2, 0)).astype(BF16),
        "w_o": w_o[0].astype(BF16), "g_ffn": g_ffn[0][None, :],
        "w_r": jnp.concatenate([w_router_group[0], w_router_expert[0],
                                jnp.zeros((d, LANES - N_GROUPS - N_GROUPS * EXPERTS_PER_GROUP), F32)],
                               axis=1).astype(BF16),
    }
    wuv = jnp.transpose(w_uv[0], (1, 0, 2)).astype(BF16)

    cs_p, sn_p = _rope_tables(jnp.arange(seq))
    cs_s, sn_s = _rope_tables(past_len + jnp.arange(t_new))
    cs_s, sn_s = jnp.tile(cs_s, (db, 1)), jnp.tile(sn_s, (db, 1))
    tiles_per_seq = seq // tm

    (c_p, kpe_p, q_p, rq_p, rk_p, rv_p, rg_p, kt, cb) = _inproj(
        x_prompt.reshape(tp, d), cs_p, sn_p, lambda i: i % tiles_per_seq, w, tm=tm, prompt=True)
    om_p = _attn_prompt(q_p, kt, cb, wuv, batch=batch, seq=seq, tq=TQ_ATTN, tk=tm)
    or_p, st_p = _retention(rq_p, rk_p, rv_p, rg_p, jnp.zeros((batch, HEADS, RET_DK, RET_DV), F32), g_ret[0],
                            seqs=1, t=PAGE, nc=seq // PAGE)

    (c_s, kpe_s, q_s, rq_s, rk_s, rv_s, rg_s) = _inproj(
        x_sample.reshape(ts, d), cs_s, sn_s, lambda i: i, w, tm=min(TM_PROJ_SAMPLE, ts), prompt=False)
    kpet_pool = jnp.swapaxes(cache_kpe.reshape(n_pool, PAGE, ROPE), 1, 2)
    o_lat_s = _attn_sample(page_table, q_s, c_s, kpe_s, cache_latent.reshape(n_pool, PAGE, KVL), kpet_pool,
                           t_new=t_new, pages=pages, sub=sub)
    om_s = _uv_sample(o_lat_s, wuv, t_new=t_new)
    or_s, st_s = _retention(rq_s, rk_s, rv_s, rg_s, state_ret[0], g_ret[0], seqs=seqs, t=t_new, nc=1)

    x2, hn, route = _outproj(x_prompt.reshape(tp, d), x_sample.reshape(ts, d), om_p, om_s, or_p, or_s, w, tm=tm)
    n_tok = tp + ts
    expert = jnp.concatenate([route[:, 2], route[:, 3]]).astype(jnp.int32)
    pos, src, tile_expert, tile_valid, tile_first = _sort_by_expert(expert, n_tok, TM_MOE)
    moe_sorted = _moe(src, tile_expert, tile_valid, tile_first, hn,
                      jnp.swapaxes(w_exp_gate, 2, 3), jnp.swapaxes(w_exp_up, 2, 3), w_exp_down, tmm=TM_MOE)
    gfin = g_final[None, :]
    tmf = min(TM_FINAL, ts)
    pos_p = jnp.concatenate([pos[:tp], pos[n_tok:n_tok + tp]])
    pos_s = jnp.concatenate([pos[tp:n_tok], pos[n_tok + tp:]])
    y_p = _final(pos_p, x2, route, moe_sorted, gfin, tm=tmf, tile0=0)
    y_s = _final(pos_s, x2, route, moe_sorted, gfin, tm=tmf, tile0=tp // tmf)

    return (y_p.reshape(batch, seq, d), y_s.reshape(db, t_new, d),
            c_p.reshape(1, batch, seq, KVL), kpe_p.reshape(1, batch, seq, ROPE), st_p[None],
            c_s.reshape(1, db, t_new, KVL), kpe_s.reshape(1, db, t_new, ROPE), st_s[None])
```
